```python
import math, functools
import jax, jax.numpy as jnp
from jax import lax
import numpy as np

D_MODEL = 4096
BATCH = 4
SEQ = 2048
DEPTH = 4
DEC_BATCH = 8
DEC_SEQ = 4
PAST_LEN = 8192
PAGE_SIZE = 128

A_HEADS = 8
A_KEY_DIM = 128
A_VAL_DIM = 128
A_KEY_WIDTH = A_HEADS * A_KEY_DIM
A_WIDTH = A_HEADS * A_VAL_DIM
A_CHUNK = 16
B_WIDTH = D_MODEL // 4
CONV_WIDTH = 3
C_HEADS = 16
C_HEAD_DIM = 128
C_WIDTH = C_HEADS * C_HEAD_DIM
MOBA_BLOCK = 256
MOBA_TOPK = 3
MOBA_QBLOCK = 16
REL_BUCKETS = 32
REL_MAX_DIST = 128
N_BRANCH = 3
D_FF = 4 * D_MODEL
NORM_EPS = 1e-6
NEG_BIG = -1e30
LB_FLOOR = 1e-30

IN_SIZES = (A_KEY_WIDTH, A_KEY_WIDTH, A_WIDTH, A_WIDTH,
            B_WIDTH, B_WIDTH, B_WIDTH,
            C_WIDTH, C_WIDTH, C_WIDTH,
            N_BRANCH * D_MODEL)

kernel_name = 'hgrn2_shortconv_moba_parallel_decoder_step'


def rms_norm(x, g):
    xf = x.astype(jnp.float32)
    y = xf * lax.rsqrt(jnp.mean(xf * xf, axis=-1, keepdims=True) + NORM_EPS)
    return (y * g.astype(jnp.float32)).astype(x.dtype)


def hgrn_lower_bounds(lb_logits):
    p = jax.nn.softmax(lb_logits.astype(jnp.float32), axis=0)
    return jnp.clip(jnp.cumsum(p, axis=0) - p[0], 0.0, 1.0)


def hgrn2_recurrence(q, k, v, log_f, s0):
    b, t, h, dk = q.shape
    dv = v.shape[-1]
    L = math.gcd(t, A_CHUNK)
    n = t // L
    rs = lambda z: z.reshape(b, n, L, h, z.shape[-1])
    q, k, v, log_f = rs(q), rs(k), rs(v), rs(log_f)
    cum = jnp.cumsum(log_f, axis=2)
    last = cum[:, :, -1:]
    ref = cum[:, :, L // 2:L // 2 + 1]
    q_rel = q * jnp.exp(cum - ref)
    k_rel = k * jnp.exp(ref - cum)
    causal = jnp.tril(jnp.ones((L, L), dtype=bool))
    a = jnp.einsum('bnlhd,bnmhd->bnhlm', q_rel, k_rel)
    a = jnp.where(causal, a, 0.0)
    o_intra = jnp.einsum('bnhlm,bnmhv->bnlhv', a, v)
    u = jnp.einsum('bnlhd,bnlhv->bnhdv', k * jnp.exp(last - cum), v)
    decay = jnp.exp(last[:, :, 0])

    def step(s, inp):
        dec, uu = inp
        return dec[..., None] * s + uu, s

    s_final, s_starts = lax.scan(step, s0, (jnp.moveaxis(decay, 1, 0), jnp.moveaxis(u, 1, 0)))
    s_starts = jnp.moveaxis(s_starts, 0, 1)
    o_inter = jnp.einsum('bnlhd,bnhdv->bnlhv', q * jnp.exp(cum), s_starts)
    return (o_intra + o_inter).reshape(b, t, h, dv), s_final


def mixer_hgrn(a_q, a_f, a_i, a_g, lb, norm_g, s0):
    bsz, t = a_q.shape[:2]
    hs = lambda z, d: z.reshape(bsz, t, A_HEADS, d).astype(jnp.float32)
    q = jax.nn.silu(hs(a_q, A_KEY_DIM))
    z = hs(a_f, A_KEY_DIM)
    lbh = lb.reshape(A_HEADS, A_KEY_DIM)
    log_f = jnp.logaddexp(jnp.log(jnp.maximum(lbh, LB_FLOOR)), jnp.log1p(-lbh) + jax.nn.log_sigmoid(z))
    k = (1.0 - lbh) * jax.nn.sigmoid(-z)
    v = hs(a_i, A_VAL_DIM)
    o, s_new = hgrn2_recurrence(q, k, v, log_f, s0.astype(jnp.float32))
    o = rms_norm(o, norm_g) * jax.nn.silu(hs(a_g, A_VAL_DIM))
    return o.reshape(bsz, t, A_WIDTH).astype(a_q.dtype), s_new


def mixer_shortconv(b_gate, c_gate, x_in, conv_w, prev):
    u = c_gate * x_in
    up = jnp.concatenate([prev.astype(u.dtype), u], axis=1)
    y = lax.conv_general_dilated(up, conv_w[:, None, :].astype(u.dtype), window_strides=(1,),
                                 padding='VALID', dimension_numbers=('NWC', 'WIO', 'NWC'),
                                 feature_group_count=B_WIDTH)
    return b_gate * y, up[:, -(CONV_WIDTH - 1):]


def rel_bucket(rel):
    n = jnp.maximum(rel, 0)
    max_exact = REL_BUCKETS // 2
    nf = jnp.maximum(n, max_exact).astype(jnp.float32)
    large = max_exact + (jnp.log(nf / max_exact) / math.log(REL_MAX_DIST / max_exact)
                         * (REL_BUCKETS - max_exact)).astype(jnp.int32)
    large = jnp.clip(large, max_exact, REL_BUCKETS - 1)
    return jnp.where(n < max_exact, n, large)


def moba_attend(q, q_pos, k_own, v_own, own_pos, rel_bias,
                k_sel=None, v_sel=None, sel_pos=None, sel_valid=None):
    f32 = jnp.float32
    scale = C_HEAD_DIM ** -0.5
    qf = q.astype(f32)
    bias_hb = rel_bias.astype(f32).T
    rel_o = q_pos[:, None] - own_pos[None, :]
    l_own = jnp.einsum('bqhd,bkhd->bhqk', qf, k_own.astype(f32)) * scale
    l_own = l_own + jnp.take(bias_hb, rel_bucket(rel_o), axis=1, mode='clip')[None]
    l_own = jnp.where(rel_o[None, None] >= 0, l_own, NEG_BIG)
    if k_sel is None:
        p_own = jax.nn.softmax(l_own, axis=-1)
        return jnp.einsum('bhqk,bkhd->bqhd', p_own, v_own.astype(f32)).astype(q.dtype)
    bsz, h, nq, r, blk = sel_pos.shape
    hi = jnp.arange(h)[None, :, None, None, None]
    rel_s = q_pos[None, None, :, None, None] - sel_pos
    l_sel = jnp.einsum('bqhd,bhqrkd->bhqrk', qf, k_sel.astype(f32)) * scale + bias_hb[hi, rel_bucket(rel_s)]
    l_sel = jnp.where(sel_valid[..., None], l_sel, NEG_BIG).reshape(bsz, h, nq, r * blk)
    p = jax.nn.softmax(jnp.concatenate([l_sel, l_own], axis=-1), axis=-1)
    p_sel = p[..., :r * blk].reshape(bsz, h, nq, r, blk)
    p_own = p[..., r * blk:]
    out = (jnp.einsum('bhqrk,bhqrkd->bqhd', p_sel, v_sel.astype(f32))
           + jnp.einsum('bhqk,bkhd->bqhd', p_own, v_own.astype(f32)))
    return out.astype(q.dtype)


def moba_prompt(q, k, v, rel_bias):
    bsz, t, h, d = q.shape
    nb = -(-t // MOBA_BLOCK)
    pad = ((0, 0), (0, nb * MOBA_BLOCK - t), (0, 0), (0, 0))
    kb = jnp.pad(k, pad).reshape(bsz, nb, MOBA_BLOCK, h, d)
    vb = jnp.pad(v, pad).reshape(bsz, nb, MOBA_BLOCK, h, d)
    pos = jnp.arange(t)
    qblk = pos // MOBA_BLOCK
    n_sel = min(MOBA_TOPK, nb - 1)
    qs = math.gcd(t, MOBA_QBLOCK)
    nq = t // qs
    q_chunks = q.reshape(bsz, nq, qs, h, d).transpose(1, 0, 2, 3, 4)
    pos_chunks = pos.reshape(nq, qs)
    if n_sel > 0:
        means = jnp.mean(kb.astype(jnp.float32), axis=2)
        gate = jnp.einsum('bthd,bnhd->bhtn', q.astype(jnp.float32), means)
        eligible = jnp.arange(nb)[None, :] < qblk[:, None]
        gate = jnp.where(eligible[None, None], gate, NEG_BIG)
        idx = lax.top_k(gate, n_sel)[1]
        valid = jnp.arange(n_sel)[None, :] < qblk[:, None]
        kh = kb.transpose(0, 3, 1, 2, 4)
        vh = vb.transpose(0, 3, 1, 2, 4)
        idx_chunks = idx.reshape(bsz, h, nq, qs, n_sel).transpose(2, 0, 1, 3, 4)
        valid_chunks = valid.reshape(nq, qs, n_sel)
        bi = jnp.arange(bsz)[:, None, None, None]
        hi = jnp.arange(h)[None, :, None, None]

        def one(args):
            qc, pc, ic, vc = args
            ob = pc[0] // MOBA_BLOCK
            own_pos = ob * MOBA_BLOCK + jnp.arange(MOBA_BLOCK)
            k_sel = kh[bi, hi, ic]
            v_sel = vh[bi, hi, ic]
            sel_pos = ic[..., None] * MOBA_BLOCK + jnp.arange(MOBA_BLOCK)
            sel_valid = jnp.broadcast_to(vc[None, None], ic.shape)
            return moba_attend(qc, pc, kb[:, ob], vb[:, ob], own_pos, rel_bias,
                               k_sel, v_sel, sel_pos, sel_valid)

        out = lax.map(one, (q_chunks, pos_chunks, idx_chunks, valid_chunks))
    else:
        def one(args):
            qc, pc = args
            ob = pc[0] // MOBA_BLOCK
            own_pos = ob * MOBA_BLOCK + jnp.arange(MOBA_BLOCK)
            return moba_attend(qc, pc, kb[:, ob], vb[:, ob], own_pos, rel_bias)

        out = lax.map(one, (q_chunks, pos_chunks))
    return out.transpose(1, 0, 2, 3, 4).reshape(bsz, t, h, d)


def moba_sample(q, k_new, v_new, cache_k, cache_v, layer, page_table, rel_bias):
    bsz, s, h, d = q.shape
    ppb = MOBA_BLOCK // PAGE_SIZE
    n_past_blk = PAST_LEN // MOBA_BLOCK
    n_sel = min(MOBA_TOPK, n_past_blk)
    n_pages = PAST_LEN // PAGE_SIZE
    own_first = n_past_blk * ppb
    n_own = n_pages - own_first
    q_pos = PAST_LEN + jnp.arange(s)
    own_pages = page_table[:, own_first:]
    k_own = jnp.concatenate([cache_k[layer, own_pages].reshape(bsz, n_own * PAGE_SIZE, h, d).astype(k_new.dtype), k_new], axis=1)
    v_own = jnp.concatenate([cache_v[layer, own_pages].reshape(bsz, n_own * PAGE_SIZE, h, d).astype(v_new.dtype), v_new], axis=1)
    own_pos = n_past_blk * MOBA_BLOCK + jnp.arange(n_own * PAGE_SIZE + s)
    if n_sel == 0:
        return moba_attend(q, q_pos, k_own, v_own, own_pos, rel_bias)
    past_pages = page_table[:, :own_first]
    k_past = cache_k[layer, past_pages].reshape(bsz, n_past_blk, MOBA_BLOCK, h, d)
    means = jnp.mean(k_past.astype(jnp.float32), axis=2)
    gate = jnp.einsum('bshd,bnhd->bhsn', q.astype(jnp.float32), means)
    idx = lax.top_k(gate, n_sel)[1]
    bi = jnp.arange(bsz)[:, None, None, None, None]
    hi = jnp.arange(h)[None, :, None, None, None]
    pid = page_table[bi, idx[..., None] * ppb + jnp.arange(ppb)]
    k_sel = cache_k[layer, pid, :, hi].reshape(bsz, h, s, n_sel, MOBA_BLOCK, d)
    v_sel = cache_v[layer, pid, :, hi].reshape(bsz, h, s, n_sel, MOBA_BLOCK, d)
    sel_pos = idx[..., None] * MOBA_BLOCK + jnp.arange(MOBA_BLOCK)
    sel_valid = jnp.ones(idx.shape, dtype=bool)
    return moba_attend(q, q_pos, k_own, v_own, own_pos, rel_bias, k_sel, v_sel, sel_pos, sel_valid)


def decoder_layer(x, hgrn_s0, conv_prev, attend, lb, w_in, b_gate, conv_w, hgrn_norm_g,
                  w_a_up, w_b_up, w_c_up, w_o, g_pre_mix, g_post_mix, g_pre_mlp, g_post_mlp,
                  w_mlp_up, w_mlp_down):
    bsz, t, _ = x.shape
    h = rms_norm(x, g_pre_mix)
    proj = jnp.einsum('btd,dc->btc', h, w_in)
    split_points = np.cumsum(IN_SIZES)[:-1].tolist()
    a_q, a_f, a_i, a_g, b_b, b_c, b_x, c_q, c_k, c_v, gates = jnp.split(proj, split_points, axis=-1)
    ya, s_new = mixer_hgrn(a_q, a_f, a_i, a_g, lb, hgrn_norm_g, hgrn_s0)
    yb, conv_new = mixer_shortconv(b_b, b_c, b_x, conv_w, conv_prev)
    heads = lambda z: z.reshape(bsz, t, C_HEADS, C_HEAD_DIM)
    kh, vh = heads(c_k), heads(c_v)
    yc = attend(heads(c_q), kh, vh).reshape(bsz, t, C_WIDTH)
    g = jax.nn.sigmoid((gates + b_gate).astype(jnp.float32)).astype(x.dtype).reshape(bsz, t, N_BRANCH, D_MODEL)
    merged = (g[:, :, 0] * (ya @ w_a_up) + g[:, :, 1] * (yb @ w_b_up) + g[:, :, 2] * (yc @ w_c_up))
    x = x + rms_norm(merged @ w_o, g_post_mix)
    hm = rms_norm(x, g_pre_mlp)
    x = x + rms_norm(jnp.square(jax.nn.relu(hm @ w_mlp_up)) @ w_mlp_down, g_post_mlp)
    return x, s_new, conv_new, kh, vh


def setup_inputs(seed: int = 0) -> dict:
    key = jax.random.key(seed)
    ks = jax.random.split(key, 24)
    f32 = jnp.float32
    nrm = lambda k, shape, sc: sc * jax.random.normal(k, shape, f32)
    n_pages = PAST_LEN // PAGE_SIZE
    n_used = DEC_BATCH * n_pages
    n_pool = n_used + (n_used + 3) // 4
    in_cols = sum(IN_SIZES)
    page_table = jax.random.permutation(ks[6], n_pool)[:n_used].reshape(DEC_BATCH, n_pages).astype(jnp.int32)
    return {
        'x_prompt': nrm(ks[0], (BATCH, SEQ, D_MODEL), 1.0),
        'x_sample': nrm(ks[1], (DEC_BATCH, DEC_SEQ, D_MODEL), 1.0),
        'cache_k': nrm(ks[2], (DEPTH, n_pool, PAGE_SIZE, C_HEADS, C_HEAD_DIM), 1.0),
        'cache_v': nrm(ks[3], (DEPTH, n_pool, PAGE_SIZE, C_HEADS, C_HEAD_DIM), 1.0),
        'state_hgrn': nrm(ks[4], (DEPTH, DEC_BATCH, A_HEADS, A_KEY_DIM, A_VAL_DIM), 0.5),
        'state_conv': nrm(ks[5], (DEPTH, DEC_BATCH, CONV_WIDTH - 1, B_WIDTH), 1.0),
        'page_table': page_table,
        'w_in': nrm(ks[7], (DEPTH, D_MODEL, in_cols), D_MODEL ** -0.5),
        'b_gate': nrm(ks[8], (DEPTH, N_BRANCH * D_MODEL), 0.1),
        'conv_w': nrm(ks[9], (DEPTH, CONV_WIDTH, B_WIDTH), CONV_WIDTH ** -0.5),
        'hgrn_lb_logits': nrm(ks[10], (DEPTH, A_KEY_WIDTH), 0.5),
        'hgrn_norm_g': 1.0 + nrm(ks[11], (DEPTH, A_VAL_DIM), 0.02),
        'w_a_up': nrm(ks[12], (DEPTH, A_WIDTH, D_MODEL), A_WIDTH ** -0.5),
        'w_b_up': nrm(ks[13], (DEPTH, B_WIDTH, D_MODEL), B_WIDTH ** -0.5),
        'w_c_up': nrm(ks[14], (DEPTH, C_WIDTH, D_MODEL), C_WIDTH ** -0.5),
        'w_o': nrm(ks[15], (DEPTH, D_MODEL, D_MODEL), D_MODEL ** -0.5),
        'rel_bias': nrm(ks[16], (REL_BUCKETS, C_HEADS), 0.5),
        'g_pre_mix': 1.0 + nrm(ks[17], (DEPTH, D_MODEL), 0.02),
        'g_post_mix': 1.0 + nrm(ks[18], (DEPTH, D_MODEL), 0.02),
        'g_pre_mlp': 1.0 + nrm(ks[19], (DEPTH, D_MODEL), 0.02),
        'g_post_mlp': 1.0 + nrm(ks[20], (DEPTH, D_MODEL), 0.02),
        'w_mlp_up': nrm(ks[21], (DEPTH, D_MODEL, D_FF), D_MODEL ** -0.5),
        'w_mlp_down': nrm(ks[22], (DEPTH, D_FF, D_MODEL), D_FF ** -0.5),
    }


def reference(x_prompt, x_sample, cache_k, cache_v, state_hgrn, state_conv, page_table,
              w_in, b_gate, conv_w, hgrn_lb_logits, hgrn_norm_g, w_a_up, w_b_up, w_c_up, w_o,
              rel_bias, g_pre_mix, g_post_mix, g_pre_mlp, g_post_mlp, w_mlp_up, w_mlp_down):
    lbs = hgrn_lower_bounds(hgrn_lb_logits)
    attend_prompt = functools.partial(moba_prompt, rel_bias=rel_bias)
    s0_prompt = jnp.zeros((BATCH, A_HEADS, A_KEY_DIM, A_VAL_DIM), jnp.float32)
    conv0_prompt = jnp.zeros((BATCH, CONV_WIDTH - 1, B_WIDTH), x_prompt.dtype)
    yp, ys = x_prompt, x_sample
    kp_l, vp_l, ks_l, vs_l, sp_l, ss_l, cp_l, cs_l = [], [], [], [], [], [], [], []
    for l in range(DEPTH):
        weights = (lbs[l], w_in[l], b_gate[l], conv_w[l], hgrn_norm_g[l], w_a_up[l], w_b_up[l],
                   w_c_up[l], w_o[l], g_pre_mix[l], g_post_mix[l], g_pre_mlp[l], g_post_mlp[l],
                   w_mlp_up[l], w_mlp_down[l])
        yp, sp, cp, kp, vp = decoder_layer(yp, s0_prompt, conv0_prompt, attend_prompt, *weights)
        attend_sample = functools.partial(moba_sample, cache_k=cache_k, cache_v=cache_v, layer=l,
                                          page_table=page_table, rel_bias=rel_bias)
        ys, ss, cs, ksm, vsm = decoder_layer(ys, state_hgrn[l], state_conv[l], attend_sample, *weights)
        kp_l.append(kp); vp_l.append(vp); ks_l.append(ksm); vs_l.append(vsm)
        sp_l.append(sp.astype(state_hgrn.dtype)); ss_l.append(ss.astype(state_hgrn.dtype))
        cp_l.append(cp.astype(state_conv.dtype)); cs_l.append(cs.astype(state_conv.dtype))
    return (yp, ys, jnp.stack(kp_l), jnp.stack(vp_l), jnp.stack(ks_l), jnp.stack(vs_l),
            jnp.stack(sp_l), jnp.stack(ss_l), jnp.stack(cp_l), jnp.stack(cs_l))
```

```python
import functools
import math

import numpy as np
import jax
import jax.numpy as jnp
from jax import lax
from jax.experimental import pallas as pl
from jax.experimental.pallas import tpu as pltpu

F32 = jnp.float32
BF16 = jnp.bfloat16

D_MODEL = 4096
BATCH = 4
SEQ = 2048
DEPTH = 4
DEC_BATCH = 8
DEC_SEQ = 4
PAST_LEN = 8192
PAGE_SIZE = 128
A_HEADS = 8
A_DIM = 128
A_WIDTH = A_HEADS * A_DIM
A_CHUNK = 16
B_WIDTH = D_MODEL // 4
C_HEADS = 16
C_DIM = 128
C_WIDTH = C_HEADS * C_DIM
MOBA_BLOCK = 256
MOBA_TOPK = 3
REL_BUCKETS = 32
REL_MAX_DIST = 128
D_FF = 4 * D_MODEL
NORM_EPS = 1e-6
NEG_BIG = -1e30
LB_FLOOR = 1e-30

N_TOK_P = BATCH * SEQ
N_TOK_S = DEC_BATCH * DEC_SEQ
N_MAIN = 4 * A_WIDTH + 3 * B_WIDTH + 3 * C_WIDTH
N_GATE = 3 * D_MODEL
OFF_AQ, OFF_AF, OFF_AI, OFF_AG = 0, 8, 16, 24
OFF_BB, OFF_BC, OFF_BX = 32, 40, 48
OFF_CQ, OFF_CK, OFF_CV = 56, 72, 88
N_PAGES = PAST_LEN // PAGE_SIZE
N_PAST_BLK = PAST_LEN // MOBA_BLOCK
PAGES_PER_BLK = MOBA_BLOCK // PAGE_SIZE
FRAME = 128

V7X_VMEM_LIMIT = 48 * 1024 * 1024

_NT = (((1,), (1,)), ((), ()))


def _cparams(sem, vmem=V7X_VMEM_LIMIT):
    return pltpu.CompilerParams(dimension_semantics=sem, vmem_limit_bytes=vmem)


def _sigmoid(x):
    return 1.0 / (1.0 + jnp.exp(-x))


def _split3(x):
    hi = x.astype(BF16)
    r1 = x - hi.astype(F32)
    lo = r1.astype(BF16)
    lo2 = (r1 - lo.astype(F32)).astype(BF16)
    return hi, lo, lo2


def _split2(x):
    hi = x.astype(BF16)
    lo = (x - hi.astype(F32)).astype(BF16)
    return hi, lo


def _rel_bucket_np(rel):
    n = np.maximum(rel, 0)
    max_exact = REL_BUCKETS // 2
    nf = np.maximum(n, max_exact).astype(np.float32)
    large = max_exact + (np.log(nf / max_exact) / math.log(REL_MAX_DIST / max_exact)
                         * (REL_BUCKETS - max_exact)).astype(np.int32)
    large = np.clip(large, max_exact, REL_BUCKETS - 1)
    return np.where(n < max_exact, n, large).astype(np.int32)


def _chunk_mats(n, chunk):
    r = np.arange(n)[:, None]
    c = np.arange(n)[None, :]
    same = (r // chunk) == (c // chunk)
    t_cum = same & (c <= r)
    t_last = same
    t_ref = same & ((c % chunk) <= chunk // 2)
    m3 = np.concatenate([t_cum, t_last, t_ref], axis=0).astype(np.float32)
    return m3, t_cum.astype(np.float32)


@functools.lru_cache(maxsize=None)
def _constants():
    m3_p, causal_p = _chunk_mats(MOBA_BLOCK, A_CHUNK)
    rows = np.arange(16 * A_DIM)[:, None]
    cols = np.arange(256)[None, :]
    emask_p = ((rows // A_DIM) == (cols // A_CHUNK)).astype(np.float32)
    m3_s, causal_s = _chunk_mats(FRAME, DEC_SEQ)
    rows = np.arange(DEC_BATCH * A_DIM)[:, None]
    cols = np.arange(FRAME)[None, :]
    emask_s = (((rows // A_DIM) == (cols // DEC_SEQ)) & (cols < N_TOK_S)).astype(np.float32)
    kl = np.arange(MOBA_BLOCK)[:, None]
    ql = np.arange(MOBA_BLOCK)[None, :]
    bucket_own = _rel_bucket_np(ql - kl)
    bucket_prev = _rel_bucket_np(MOBA_BLOCK + ql - kl)
    s = np.arange(8)[:, None]
    t = np.arange(MOBA_BLOCK)[None, :]
    bucket_s = _rel_bucket_np(MOBA_BLOCK + s - t)
    return dict(m3_p=m3_p, causal_p=causal_p, emask_p=emask_p, m3_s=m3_s, causal_s=causal_s,
                emask_s=emask_s, bucket_own=bucket_own, bucket_prev=bucket_prev, bucket_s=bucket_s)


def _mm_body(*refs, nk, epilogue):
    if epilogue == "sigmoid_bias":
        a_ref, w_ref, b_ref, o_ref = refs[:4]
        rest = refs[4:]
    else:
        a_ref, w_ref, o_ref = refs[:3]
        b_ref = None
        rest = refs[3:]

    def finish(r):
        if epilogue == "sigmoid_bias":
            r = _sigmoid(r + b_ref[...])
        elif epilogue == "relu2":
            r = jnp.square(jnp.maximum(r, 0.0))
        o_ref[...] = r.astype(o_ref.dtype)

    prod = jnp.dot(a_ref[...], w_ref[...], preferred_element_type=F32)
    if nk == 1:
        finish(prod)
        return
    acc_ref = rest[0]
    k = pl.program_id(2)

    @pl.when(k == 0)
    def _():
        acc_ref[...] = prod

    @pl.when(k > 0)
    def _():
        acc_ref[...] += prod

    @pl.when(k == nk - 1)
    def _():
        finish(acc_ref[...])


def _matmul(a, w, layer, *, col0=0, n_cols=None, tm, tn, tk, epilogue="none", bias=None,
            out_dtype=F32, name):
    m, kdim = a.shape
    n_cols = w.shape[2] if n_cols is None else n_cols
    assert m % tm == 0 and kdim % tk == 0 and n_cols % tn == 0 and col0 % tn == 0
    nk = kdim // tk
    cb = col0 // tn
    in_specs = [pl.BlockSpec((tm, tk), lambda i, j, k: (i, k)),
                pl.BlockSpec((None, tk, tn), lambda i, j, k: (layer, k, j + cb))]
    args = [a, w]
    if epilogue == "sigmoid_bias":
        in_specs.append(pl.BlockSpec((None, 1, tn), lambda i, j, k: (layer, 0, j)))
        args.append(bias)
    scratch = [] if nk == 1 else [pltpu.VMEM((tm, tn), F32)]
    return pl.pallas_call(
        functools.partial(_mm_body, nk=nk, epilogue=epilogue),
        out_shape=jax.ShapeDtypeStruct((m, n_cols), out_dtype),
        grid=(m // tm, n_cols // tn, nk),
        in_specs=in_specs,
        out_specs=pl.BlockSpec((tm, tn), lambda i, j, k: (i, j)),
        scratch_shapes=scratch,
        compiler_params=_cparams(("parallel", "parallel", "arbitrary")),
        name=name,
    )(*args)


def _rms(x, g):
    return x * lax.rsqrt(jnp.mean(x * x, axis=-1, keepdims=True) + NORM_EPS) * g


def _prenorm_body(x_ref, g_ref, h_ref):
    h_ref[...] = _rms(x_ref[...], g_ref[...]).astype(h_ref.dtype)


def _prenorm(x, g, layer, *, tm):
    m = x.shape[0]
    return pl.pallas_call(
        _prenorm_body,
        out_shape=jax.ShapeDtypeStruct((m, D_MODEL), BF16),
        grid=(m // tm,),
        in_specs=[pl.BlockSpec((tm, D_MODEL), lambda i: (i, 0)),
                  pl.BlockSpec((None, 1, D_MODEL), lambda i: (layer, 0, 0))],
        out_specs=pl.BlockSpec((tm, D_MODEL), lambda i: (i, 0)),
        compiler_params=_cparams(("parallel",)),
        name="prenorm",
    )(x, g)


def _post_body(x_ref, y_ref, gpost_ref, *rest, with_next):
    xn = x_ref[...] + _rms(y_ref[...], gpost_ref[...])
    if with_next:
        gnext_ref, xo_ref, ho_ref = rest
        ho_ref[...] = _rms(xn, gnext_ref[...]).astype(ho_ref.dtype)
    else:
        (xo_ref,) = rest
    xo_ref[...] = xn


def _post(x, y, gpost, layer, gnext=None, next_layer=None, *, tm):
    m = x.shape[0]
    with_next = gnext is not None
    row = pl.BlockSpec((tm, D_MODEL), lambda i: (i, 0))
    in_specs = [row, row, pl.BlockSpec((None, 1, D_MODEL), lambda i: (layer, 0, 0))]
    args = [x, y, gpost]
    out_shape = [jax.ShapeDtypeStruct((m, D_MODEL), F32)]
    out_specs = [row]
    if with_next:
        in_specs.append(pl.BlockSpec((None, 1, D_MODEL), lambda i: (next_layer, 0, 0)))
        args.append(gnext)
        out_shape.append(jax.ShapeDtypeStruct((m, D_MODEL), BF16))
        out_specs.append(row)
    res = pl.pallas_call(
        functools.partial(_post_body, with_next=with_next),
        out_shape=out_shape,
        grid=(m // tm,),
        in_specs=in_specs,
        out_specs=out_specs,
        compiler_params=_cparams(("parallel",)),
        name="post_norm",
    )(*args)
    return (res[0], res[1]) if with_next else (res[0], None)


def _lb_body(x_ref, o_ref):
    x = x_ref[...]
    rows = [x[l:l + 1] for l in range(DEPTH)]
    mx = functools.reduce(jnp.maximum, rows)
    es = [jnp.exp(r - mx) for r in rows]
    tot = functools.reduce(lambda a, b: a + b, es)
    ps = [e / tot for e in es]
    cum = None
    for l in range(DEPTH):
        cum = ps[l] if cum is None else cum + ps[l]
        lb = jnp.clip(cum - ps[0], 0.0, 1.0)
        o_ref[3 * l:3 * l + 1, :] = jnp.log(jnp.maximum(lb, LB_FLOOR))
        o_ref[3 * l + 1:3 * l + 2, :] = jnp.log1p(-lb)
        o_ref[3 * l + 2:3 * l + 3, :] = 1.0 - lb


def _lower_bound_params(lb_logits):
    out = pl.pallas_call(
        _lb_body,
        out_shape=jax.ShapeDtypeStruct((3 * DEPTH, A_WIDTH), F32),
        name="hgrn_lower_bounds",
    )(lb_logits)
    return out.reshape(DEPTH, 3, A_HEADS, 1, A_DIM)


def _hgrn_gates(aq, z, la, l1m, oml):
    q = aq * _sigmoid(aq)
    log_sig = jnp.minimum(z, 0.0) - jnp.log1p(jnp.exp(-jnp.abs(z)))
    b = l1m + log_sig
    log_f = jnp.maximum(la, b) + jnp.log1p(jnp.exp(-jnp.abs(la - b)))
    k = oml / (1.0 + jnp.exp(z))
    return q, k, log_f


def _chunk_sums(m3, log_f, n):
    hi, lo, lo2 = _split3(log_f)
    c3 = (jnp.dot(m3, hi, preferred_element_type=F32)
          + jnp.dot(m3, lo, preferred_element_type=F32)
          + jnp.dot(m3, lo2, preferred_element_type=F32))
    return c3[0:n], c3[n:2 * n], c3[2 * n:3 * n]


def _hgrn_prompt_body(aq_ref, af_ref, ai_ref, ag_ref, lbp_ref, ng_ref, m3_ref, cmask_ref, emask_ref,
                      ya_ref, st_ref, o_scr, qc_scr, dec_scr, ut_scr):
    blk = MOBA_BLOCK
    n_blk = SEQ // blk
    cpb = blk // A_CHUNK
    la = lbp_ref[0]
    l1m = lbp_ref[1]
    oml = lbp_ref[2]

    def block_body(j, carry):
        rows = pl.ds(pl.multiple_of(j * blk, blk), blk)
        q, k, log_f = _hgrn_gates(aq_ref[rows, :], af_ref[rows, :], la, l1m, oml)
        v = ai_ref[rows, :]
        cum, last_b, ref_b = _chunk_sums(m3_ref[...], log_f, blk)
        q_rel = (q * jnp.exp(cum - ref_b)).astype(BF16)
        k_rel = (k * jnp.exp(ref_b - cum)).astype(BF16)
        k_last = (k * jnp.exp(last_b - cum)).astype(BF16)
        a = lax.dot_general(q_rel, k_rel, _NT, preferred_element_type=F32)
        a = jnp.where(cmask_ref[...] > 0.0, a, 0.0)
        o_scr[rows, :] = jnp.dot(a.astype(BF16), v.astype(BF16), preferred_element_type=F32)
        qc_scr[rows, :] = (q * jnp.exp(cum)).astype(BF16)
        dec_scr[rows, :] = jnp.exp(last_b)
        vt = v.T.astype(BF16)
        vt_exp = jnp.concatenate([vt] * cpb, axis=0) * emask_ref[...]
        ut_rows = pl.ds(pl.multiple_of(j * (cpb * A_DIM), cpb * A_DIM), cpb * A_DIM)
        ut_scr[ut_rows, :] = jnp.dot(vt_exp, k_last, preferred_element_type=F32)
        return carry

    lax.fori_loop(0, n_blk, block_body, 0)

    def scan_body(n, st):
        rows = pl.ds(pl.multiple_of(n * A_DIM, A_DIM), A_DIM)
        u = ut_scr[rows, :]
        ut_scr[rows, :] = st
        d = dec_scr[pl.ds(pl.multiple_of(n * A_CHUNK, A_CHUNK), 8), :][0:1, :]
        return st * d + u

    st = lax.fori_loop(0, SEQ // A_CHUNK, scan_body, jnp.zeros((A_DIM, A_DIM), F32))
    st_ref[...] = st.T

    ng = ng_ref[...]

    def out_body(j, carry):
        rows = pl.ds(pl.multiple_of(j * blk, blk), blk)
        ut_rows = pl.ds(pl.multiple_of(j * (cpb * A_DIM), cpb * A_DIM), cpb * A_DIM)
        starts = ut_scr[ut_rows, :].astype(BF16)
        r = lax.dot_general(qc_scr[rows, :], starts, _NT, preferred_element_type=F32)
        o_inter = jnp.concatenate(
            [r[n * A_CHUNK:(n + 1) * A_CHUNK, n * A_DIM:(n + 1) * A_DIM] for n in range(cpb)], axis=0)
        o = _rms(o_scr[rows, :] + o_inter, ng)
        ag = ag_ref[rows, :]
        ya_ref[rows, :] = (o * (ag * _sigmoid(ag))).astype(ya_ref.dtype)
        return carry

    lax.fori_loop(0, n_blk, out_body, 0)


def _hgrn_prompt(proj, lbp, norm_g, layer, consts):
    col = lambda off: pl.BlockSpec((SEQ, A_DIM), lambda b, h: (b, off + h))
    whole = lambda shape: pl.BlockSpec(shape, lambda b, h: (0,) * len(shape))
    return pl.pallas_call(
        _hgrn_prompt_body,
        out_shape=[jax.ShapeDtypeStruct((N_TOK_P, A_WIDTH), BF16),
                   jax.ShapeDtypeStruct((BATCH, A_HEADS, A_DIM, A_DIM), F32)],
        grid=(BATCH, A_HEADS),
        in_specs=[col(OFF_AQ), col(OFF_AF), col(OFF_AI), col(OFF_AG),
                  pl.BlockSpec((None, 3, None, 1, A_DIM), lambda b, h: (layer, 0, h, 0, 0)),
                  pl.BlockSpec((None, 1, A_DIM), lambda b, h: (layer, 0, 0)),
                  whole((3 * MOBA_BLOCK, MOBA_BLOCK)), whole((MOBA_BLOCK, MOBA_BLOCK)),
                  whole((16 * A_DIM, MOBA_BLOCK))],
        out_specs=[pl.BlockSpec((SEQ, A_DIM), lambda b, h: (b, h)),
                   pl.BlockSpec((None, None, A_DIM, A_DIM), lambda b, h: (b, h, 0, 0))],
        scratch_shapes=[pltpu.VMEM((SEQ, A_DIM), F32), pltpu.VMEM((SEQ, A_DIM), BF16),
                        pltpu.VMEM((SEQ, A_DIM), F32), pltpu.VMEM((SEQ // A_CHUNK * A_DIM, A_DIM), F32)],
        compiler_params=_cparams(("parallel", "parallel")),
        name="hgrn_prompt",
    )(proj, proj, proj, proj, lbp, norm_g, consts["m3_p"], consts["causal_p"], consts["emask_p"])


def _pad_frame(x):
    return jnp.concatenate([x, jnp.zeros((FRAME - x.shape[0], x.shape[1]), x.dtype)], axis=0)


def _hgrn_sample_body(aq_ref, af_ref, ai_ref, ag_ref, s0_ref, lbp_ref, ng_ref, m3_ref, cmask_ref,
                      emask_ref, ya_ref, sn_ref):
    la = lbp_ref[0]
    l1m = lbp_ref[1]
    oml = lbp_ref[2]
    q, k, log_f = _hgrn_gates(_pad_frame(aq_ref[...]), _pad_frame(af_ref[...]), la, l1m, oml)
    v = _pad_frame(ai_ref[...])
    cum, last_b, ref_b = _chunk_sums(m3_ref[...], log_f, FRAME)
    q_rel = (q * jnp.exp(cum - ref_b)).astype(BF16)
    k_rel = (k * jnp.exp(ref_b - cum)).astype(BF16)
    k_last = (k * jnp.exp(last_b - cum)).astype(BF16)
    a = lax.dot_general(q_rel, k_rel, _NT, preferred_element_type=F32)
    a = jnp.where(cmask_ref[...] > 0.0, a, 0.0)
    o_intra = jnp.dot(a.astype(BF16), v.astype(BF16), preferred_element_type=F32)
    qc = (q * jnp.exp(cum)).astype(BF16)
    dec = jnp.exp(last_b)
    vt = v.T.astype(BF16)
    vt_exp = jnp.concatenate([vt] * DEC_BATCH, axis=0) * emask_ref[...]
    ut = jnp.dot(vt_exp, k_last, preferred_element_type=F32)
    s0t = jnp.concatenate([s0_ref[b].T for b in range(DEC_BATCH)], axis=0)
    r = lax.dot_general(qc, s0t.astype(BF16), _NT, preferred_element_type=F32)
    row_b = lax.broadcasted_iota(jnp.int32, (FRAME, A_DIM), 0) >> int(math.log2(DEC_SEQ))
    o_inter = jnp.zeros((FRAME, A_DIM), F32)
    for b in range(DEC_BATCH):
        o_inter = jnp.where(row_b == b, r[:, b * A_DIM:(b + 1) * A_DIM], o_inter)
        d = dec[b * DEC_SEQ:b * DEC_SEQ + 1, :]
        snt = s0t[b * A_DIM:(b + 1) * A_DIM, :] * d + ut[b * A_DIM:(b + 1) * A_DIM, :]
        sn_ref[b] = snt.T
    o = _rms(o_intra + o_inter, ng_ref[...])
    ag = _pad_frame(ag_ref[...])
    ya_ref[...] = (o * (ag * _sigmoid(ag)))[0:N_TOK_S].astype(ya_ref.dtype)


def _hgrn_sample(proj, state, lbp, norm_g, layer, consts):
    col = lambda off: pl.BlockSpec((N_TOK_S, A_DIM), lambda h: (0, off + h))
    whole = lambda shape: pl.BlockSpec(shape, lambda h: (0,) * len(shape))
    return pl.pallas_call(
        _hgrn_sample_body,
        out_shape=[jax.ShapeDtypeStruct((N_TOK_S, A_WIDTH), BF16),
                   jax.ShapeDtypeStruct((DEC_BATCH, A_HEADS, A_DIM, A_DIM), F32)],
        grid=(A_HEADS,),
        in_specs=[col(OFF_AQ), col(OFF_AF), col(OFF_AI), col(OFF_AG),
                  pl.BlockSpec((None, DEC_BATCH, None, A_DIM, A_DIM), lambda h: (layer, 0, h, 0, 0)),
                  pl.BlockSpec((None, 3, None, 1, A_DIM), lambda h: (layer, 0, h, 0, 0)),
                  pl.BlockSpec((None, 1, A_DIM), lambda h: (layer, 0, 0)),
                  whole((3 * FRAME, FRAME)), whole((FRAME, FRAME)), whole((DEC_BATCH * A_DIM, FRAME))],
        out_specs=[pl.BlockSpec((N_TOK_S, A_DIM), lambda h: (0, h)),
                   pl.BlockSpec((DEC_BATCH, None, A_DIM, A_DIM), lambda h: (0, h, 0, 0))],
        compiler_params=_cparams(("parallel",)),
        name="hgrn_sample",
    )(proj, proj, proj, proj, state, lbp, norm_g, consts["m3_s"], consts["causal_s"], consts["emask_s"])


def _conv_body(bb_ref, bc_ref, bx_ref, w_ref, *rest, t, has_prev):
    if has_prev:
        prev_ref, y_ref, cn_ref, scr = rest
    else:
        y_ref, cn_ref, scr = rest
    u = bc_ref[...] * bx_ref[...]
    scr[0:8, :] = jnp.zeros((8, scr.shape[1]), F32)
    if has_prev:
        scr[6:8, :] = prev_ref[...]
    scr[8:8 + t, :] = u
    w = w_ref[...]
    y = w[0:1] * scr[6:6 + t, :] + w[1:2] * scr[7:7 + t, :] + w[2:3] * u
    y_ref[...] = (bb_ref[...] * y).astype(y_ref.dtype)
    cn_ref[...] = scr[6 + t:8 + t, :]


def _shortconv(proj3, conv_w, layer, prev, *, tc, out_dtype):
    bsz, t, _ = proj3.shape
    has_prev = prev is not None
    lanes = lambda off: pl.BlockSpec((None, t, tc), lambda b, c: (b, 0, off * 128 // tc + c))
    in_specs = [lanes(OFF_BB), lanes(OFF_BC), lanes(OFF_BX),
                pl.BlockSpec((None, 3, tc), lambda b, c: (layer, 0, c))]
    args = [proj3, proj3, proj3, conv_w]
    if has_prev:
        in_specs.append(pl.BlockSpec((None, None, 2, tc), lambda b, c: (layer, b, 0, c)))
        args.append(prev)
    return pl.pallas_call(
        functools.partial(_conv_body, t=t, has_prev=has_prev),
        out_shape=[jax.ShapeDtypeStruct((bsz, t, B_WIDTH), out_dtype),
                   jax.ShapeDtypeStruct((bsz, 2, B_WIDTH), F32)],
        grid=(bsz, B_WIDTH // tc),
        in_specs=in_specs,
        out_specs=[pl.BlockSpec((None, t, tc), lambda b, c: (b, 0, c)),
                   pl.BlockSpec((None, 2, tc), lambda b, c: (b, 0, c))],
        scratch_shapes=[pltpu.VMEM((t + 8, tc), F32)],
        compiler_params=_cparams(("parallel", "parallel")),
        name="shortconv",
    )(*args)


def _bias_body(rb_ref, bo_ref, bp_ref, bs_ref, own_ref, prev_ref, smp_ref):
    h = pl.program_id(0)

    def lookup(bucket):
        acc = jnp.zeros(bucket.shape, F32)
        for b in range(REL_BUCKETS):
            acc = jnp.where(bucket == b, rb_ref[b, h], acc)
        return acc

    own_ref[...] = lookup(bo_ref[...])
    prev_ref[...] = lookup(bp_ref[...])
    smp_ref[...] = lookup(bs_ref[...])


def _bias_tables(rel_bias, consts):
    whole = lambda shape: pl.BlockSpec(shape, lambda h: (0,) * len(shape))
    blk = MOBA_BLOCK
    own, prev, smp = pl.pallas_call(
        _bias_body,
        out_shape=[jax.ShapeDtypeStruct((C_HEADS, blk, blk), F32),
                   jax.ShapeDtypeStruct((C_HEADS, blk, blk), F32),
                   jax.ShapeDtypeStruct((C_HEADS, 8, blk), F32)],
        grid=(C_HEADS,),
        in_specs=[pl.BlockSpec(memory_space=pltpu.SMEM),
                  whole((blk, blk)), whole((blk, blk)), whole((8, blk))],
        out_specs=[pl.BlockSpec((None, blk, blk), lambda h: (h, 0, 0)),
                   pl.BlockSpec((None, blk, blk), lambda h: (h, 0, 0)),
                   pl.BlockSpec((None, 8, blk), lambda h: (h, 0, 0))],
        compiler_params=_cparams(("parallel",)),
        name="rel_bias_tables",
    )(rel_bias, consts["bucket_own"], consts["bucket_prev"], consts["bucket_s"])
    return own, prev, smp.reshape(C_HEADS, 8, 1, blk)


def _moba_prompt_body(rb_ref, q_ref, k_ref, v_ref, bown_ref, bprev_ref, o_ref,
                      kbf, vtbf, means_scr, sel_scr, m_scr, l_scr, acc_scr):
    blk = MOBA_BLOCK
    n_blk = SEQ // blk
    h = pl.program_id(1)
    qi = pl.program_id(2)
    scale = C_DIM ** -0.5

    @pl.when(qi == 0)
    def _():
        means_scr[...] = jnp.zeros(means_scr.shape, F32)
        for i in range(n_blk):
            kb = k_ref[i * blk:(i + 1) * blk, :]
            kbf[i * blk:(i + 1) * blk, :] = kb.astype(BF16)
            means_scr[i:i + 1, :] = jnp.sum(kb, axis=0, keepdims=True) / blk
            vtbf[:, i * blk:(i + 1) * blk] = v_ref[i * blk:(i + 1) * blk, :].T.astype(BF16)

    q = q_ref[...]
    qb = q.astype(BF16)

    mh, ml = _split2(means_scr[...])
    qh, ql = _split2(q)
    gate = (lax.dot_general(mh, qh, _NT, preferred_element_type=F32)
            + lax.dot_general(mh, ql, _NT, preferred_element_type=F32)
            + lax.dot_general(ml, qh, _NT, preferred_element_type=F32))[0:n_blk]
    blk_i = lax.broadcasted_iota(jnp.int32, (n_blk, blk), 0)
    rank = jnp.zeros((n_blk, blk), jnp.int32)
    for jp in range(n_blk):
        gj = gate[jp:jp + 1, :]
        beats = (gj > gate) | ((gj == gate) & (jp < blk_i))
        rank = rank + jnp.where(beats, 1, 0) * jnp.where(jp < qi, 1, 0)
    sel = jnp.where((blk_i < qi) & (rank < MOBA_TOPK), 1.0, 0.0)
    for jp in range(n_blk):
        sel_scr[jp] = sel[jp:jp + 1, :]

    own = pl.ds(pl.multiple_of(qi * blk, blk), blk)
    s = lax.dot_general(kbf[own, :], qb, _NT, preferred_element_type=F32) * scale + bown_ref[...]
    key_i = lax.broadcasted_iota(jnp.int32, (blk, blk), 0)
    qry_i = lax.broadcasted_iota(jnp.int32, (blk, blk), 1)
    s = jnp.where(key_i <= qry_i, s, NEG_BIG)
    m0 = jnp.max(s, axis=0, keepdims=True)
    p = jnp.exp(s - m0)
    m_scr[...] = m0
    l_scr[...] = jnp.sum(p, axis=0, keepdims=True)
    acc_scr[...] = jnp.dot(vtbf[:, own], p.astype(BF16), preferred_element_type=F32)

    def past_block(j, bias):
        rows = pl.ds(pl.multiple_of(j * blk, blk), blk)
        sj = lax.dot_general(kbf[rows, :], qb, _NT, preferred_element_type=F32) * scale + bias
        sj = jnp.where(sel_scr[j] > 0.0, sj, NEG_BIG)
        m_old = m_scr[...]
        m_new = jnp.maximum(m_old, jnp.max(sj, axis=0, keepdims=True))
        alpha = jnp.exp(m_old - m_new)
        pj = jnp.exp(sj - m_new)
        m_scr[...] = m_new
        l_scr[...] = l_scr[...] * alpha + jnp.sum(pj, axis=0, keepdims=True)
        acc_scr[...] = acc_scr[...] * alpha + jnp.dot(vtbf[:, rows], pj.astype(BF16),
                                                      preferred_element_type=F32)

    @pl.when(qi >= 1)
    def _():
        past_block(qi - 1, bprev_ref[...])

    far_bias = rb_ref[REL_BUCKETS - 1, h]

    def far_body(j, carry):
        past_block(j, far_bias)
        return carry

    lax.fori_loop(0, qi - 1, far_body, 0)

    o_ref[...] = (acc_scr[...] / l_scr[...]).T.astype(o_ref.dtype)


def _moba_prompt(proj, rel_bias, bown, bprev):
    blk = MOBA_BLOCK
    n_blk = SEQ // blk
    return pl.pallas_call(
        _moba_prompt_body,
        out_shape=jax.ShapeDtypeStruct((N_TOK_P, C_WIDTH), BF16),
        grid=(BATCH, C_HEADS, n_blk),
        in_specs=[pl.BlockSpec(memory_space=pltpu.SMEM),
                  pl.BlockSpec((blk, C_DIM), lambda b, h, i: (b * n_blk + i, OFF_CQ + h)),
                  pl.BlockSpec((SEQ, C_DIM), lambda b, h, i: (b, OFF_CK + h)),
                  pl.BlockSpec((SEQ, C_DIM), lambda b, h, i: (b, OFF_CV + h)),
                  pl.BlockSpec((None, blk, blk), lambda b, h, i: (h, 0, 0)),
                  pl.BlockSpec((None, blk, blk), lambda b, h, i: (h, 0, 0))],
        out_specs=pl.BlockSpec((blk, C_DIM), lambda b, h, i: (b * n_blk + i, h)),
        scratch_shapes=[pltpu.VMEM((SEQ, C_DIM), BF16), pltpu.VMEM((C_DIM, SEQ), BF16),
                        pltpu.VMEM((2 * n_blk, C_DIM), F32), pltpu.VMEM((n_blk, 1, blk), F32),
                        pltpu.VMEM((1, blk), F32), pltpu.VMEM((1, blk), F32),
                        pltpu.VMEM((C_DIM, blk), F32)],
        compiler_params=_cparams(("parallel", "parallel", "arbitrary")),
        name="moba_prompt",
    )(rel_bias, proj, proj, proj, bown, bprev)


def _cache_means_body(pt_ref, k0_ref, k1_ref, o_ref):
    o_ref[...] = (jnp.sum(k0_ref[...], axis=0) + jnp.sum(k1_ref[...], axis=0)) / MOBA_BLOCK


def _cache_block_means(cache_k, pt_flat):
    def page(j):
        return pl.BlockSpec((None, None, PAGE_SIZE, C_HEADS, C_DIM),
                            lambda l, b, n, pt: (l, pt[b * N_PAGES + n * PAGES_PER_BLK + j], 0, 0, 0))
    return pl.pallas_call(
        _cache_means_body,
        out_shape=jax.ShapeDtypeStruct((DEPTH, DEC_BATCH, N_PAST_BLK, C_HEADS, C_DIM), F32),
        grid_spec=pltpu.PrefetchScalarGridSpec(
            num_scalar_prefetch=1,
            grid=(DEPTH, DEC_BATCH, N_PAST_BLK),
            in_specs=[page(0), page(1)],
            out_specs=pl.BlockSpec((None, None, None, C_HEADS, C_DIM),
                                   lambda l, b, n, pt: (l, b, n, 0, 0))),
        compiler_params=_cparams(("parallel", "parallel", "parallel")),
        name="cache_block_means",
    )(pt_flat, cache_k, cache_k)


def _moba_gate_body(q_ref, means_ref, idx_ref):
    n_cand = DEC_BATCH * N_PAST_BLK
    q = q_ref[...]
    means = means_ref[...].reshape(n_cand, C_DIM)
    qh, ql = _split2(q)
    mh, ml = _split2(means)
    g = (lax.dot_general(qh, mh, _NT, preferred_element_type=F32)
         + lax.dot_general(qh, ml, _NT, preferred_element_type=F32)
         + lax.dot_general(ql, mh, _NT, preferred_element_type=F32))
    col = lax.broadcasted_iota(jnp.int32, (N_TOK_S, n_cand), 1)
    row = lax.broadcasted_iota(jnp.int32, (N_TOK_S, n_cand), 0)
    own_seq = (col >> int(math.log2(N_PAST_BLK))) == (row >> int(math.log2(DEC_SEQ)))
    g = jnp.where(own_seq, g, -jnp.inf)
    colf = col.astype(F32)
    lane = lax.broadcasted_iota(jnp.int32, (N_TOK_S, 128), 1)
    out = jnp.zeros((N_TOK_S, 128), jnp.int32)
    for r in range(MOBA_TOPK):
        mx = jnp.max(g, axis=1, keepdims=True)
        am = jnp.min(jnp.where(g == mx, colf, float(n_cand)), axis=1, keepdims=True)
        out = jnp.where(lane == r, am.astype(jnp.int32) & (N_PAST_BLK - 1), out)
        g = jnp.where(colf == am, -jnp.inf, g)
    idx_ref[...] = out


def _moba_gate(proj_s, means, layer):
    return pl.pallas_call(
        _moba_gate_body,
        out_shape=jax.ShapeDtypeStruct((C_HEADS, N_TOK_S, 128), jnp.int32),
        grid=(C_HEADS,),
        in_specs=[pl.BlockSpec((N_TOK_S, C_DIM), lambda h: (0, OFF_CQ + h)),
                  pl.BlockSpec((None, DEC_BATCH, N_PAST_BLK, C_DIM), lambda h: (layer, 0, 0, h))],
        out_specs=pl.BlockSpec((None, N_TOK_S, 128), lambda h: (h, 0, 0)),
        compiler_params=_cparams(("parallel",)),
        name="moba_gate",
    )(proj_s, means.reshape(DEPTH, DEC_BATCH, N_PAST_BLK, C_WIDTH))


def _moba_sample_body(idx_ref, pt_ref, rb_ref, q_ref, kn_ref, vn_ref, bs_ref, *rest):
    n_pg = MOBA_TOPK * PAGES_PER_BLK
    k_refs = rest[:n_pg]
    v_refs = rest[n_pg:2 * n_pg]
    o_ref = rest[2 * n_pg]
    h = pl.program_id(0)
    b = pl.program_id(1)
    s = pl.program_id(2)
    row = b * DEC_SEQ + s
    scale = C_DIM ** -0.5
    q8 = jnp.broadcast_to(q_ref[...], (8, C_DIM)).astype(BF16)
    far_bias = rb_ref[REL_BUCKETS - 1, h]
    bias_newest = bs_ref[...]

    logits = []
    for r in range(MOBA_TOPK):
        n = idx_ref[(h * N_TOK_S + row) * MOBA_TOPK + r]
        for j in range(PAGES_PER_BLK):
            kp = k_refs[r * PAGES_PER_BLK + j][...].astype(BF16)
            lg = lax.dot_general(q8, kp, _NT, preferred_element_type=F32)[0:1] * scale
            bias = jnp.where(n == N_PAST_BLK - 1,
                             bias_newest[:, j * PAGE_SIZE:(j + 1) * PAGE_SIZE], far_bias)
            logits.append(lg + bias)
    kn = _pad_frame(kn_ref[...]).astype(BF16)
    lg = lax.dot_general(q8, kn, _NT, preferred_element_type=F32)[0:1] * scale
    lane = lax.broadcasted_iota(jnp.int32, (1, FRAME), 1)
    ob = jnp.zeros((1, FRAME), F32)
    for d in range(DEC_SEQ):
        ob = jnp.where(lane == row - d, rb_ref[d, h], ob)
    valid = (lane >= b * DEC_SEQ) & (lane <= row)
    logits.append(jnp.where(valid, lg + ob, NEG_BIG))

    mx = functools.reduce(jnp.maximum, logits)
    mx = jnp.max(mx, axis=1, keepdims=True)
    ps = [jnp.exp(l - mx) for l in logits]
    denom = functools.reduce(lambda x, y: x + y, [jnp.sum(p, axis=1, keepdims=True) for p in ps])
    vals = [v_refs[i][...].astype(BF16) for i in range(n_pg)] + [_pad_frame(vn_ref[...]).astype(BF16)]
    out = jnp.zeros((8, C_DIM), F32)
    for p, vv in zip(ps, vals):
        p8 = jnp.broadcast_to(p, (8, p.shape[1])).astype(BF16)
        out = out + jnp.dot(p8, vv, preferred_element_type=F32)
    o_ref[...] = out[0:1] / denom


def _moba_sample(proj_s, cache_k, cache_v, idx_flat, pt_flat, rel_bias, bias_s, layer):
    proj_rows = proj_s.reshape(N_TOK_S, 1, N_MAIN)

    def page(r, j):
        def index_map(h, b, s, idx, pt):
            n = idx[(h * N_TOK_S + b * DEC_SEQ + s) * MOBA_TOPK + r]
            return (layer, pt[b * N_PAGES + n * PAGES_PER_BLK + j], 0, h)
        return pl.BlockSpec((None, None, PAGE_SIZE, C_DIM), index_map)

    pages = [page(r, j) for r in range(MOBA_TOPK) for j in range(PAGES_PER_BLK)]
    cache_k = cache_k.reshape(DEPTH, -1, PAGE_SIZE, C_WIDTH)
    cache_v = cache_v.reshape(DEPTH, -1, PAGE_SIZE, C_WIDTH)
    out = pl.pallas_call(
        _moba_sample_body,
        out_shape=jax.ShapeDtypeStruct((N_TOK_S, 1, C_WIDTH), F32),
        grid_spec=pltpu.PrefetchScalarGridSpec(
            num_scalar_prefetch=2,
            grid=(C_HEADS, DEC_BATCH, DEC_SEQ),
            in_specs=[pl.BlockSpec(memory_space=pltpu.SMEM),
                      pl.BlockSpec((None, 1, C_DIM), lambda h, b, s, idx, pt: (b * DEC_SEQ + s, 0, OFF_CQ + h)),
                      pl.BlockSpec((N_TOK_S, C_DIM), lambda h, b, s, idx, pt: (0, OFF_CK + h)),
                      pl.BlockSpec((N_TOK_S, C_DIM), lambda h, b, s, idx, pt: (0, OFF_CV + h)),
                      pl.BlockSpec((None, None, 1, MOBA_BLOCK), lambda h, b, s, idx, pt: (h, s, 0, 0))]
                     + pages + pages,
            out_specs=pl.BlockSpec((None, 1, C_DIM), lambda h, b, s, idx, pt: (b * DEC_SEQ + s, 0, h))),
        compiler_params=_cparams(("parallel", "parallel", "parallel")),
        name="moba_sample",
    )(idx_flat, pt_flat, rel_bias, proj_rows, proj_s, proj_s, bias_s,
      *([cache_k] * len(pages)), *([cache_v] * len(pages)))
    return out.reshape(N_TOK_S, C_WIDTH)


def _merge_body(ya_ref, yb_ref, yc_ref, wa_ref, wb_ref, wc_ref, g0_ref, g1_ref, g2_ref, o_ref):
    acc = g0_ref[...] * jnp.dot(ya_ref[...], wa_ref[...], preferred_element_type=F32)
    acc = acc + g1_ref[...] * jnp.dot(yb_ref[...], wb_ref[...], preferred_element_type=F32)
    acc = acc + g2_ref[...] * jnp.dot(yc_ref[...], wc_ref[...], preferred_element_type=F32)
    o_ref[...] = acc.astype(o_ref.dtype)


def _merge(ya, yb, yc, gates, w_a, w_b, w_c, layer, *, tm, tn):
    m = ya.shape[0]
    nb = D_MODEL // tn
    act = lambda width: pl.BlockSpec((tm, width), lambda j, i: (i, 0))
    wgt = lambda width: pl.BlockSpec((None, width, tn), lambda j, i: (layer, 0, j))
    gate = lambda br: pl.BlockSpec((tm, tn), lambda j, i: (i, br * nb + j))
    return pl.pallas_call(
        _merge_body,
        out_shape=jax.ShapeDtypeStruct((m, D_MODEL), BF16),
        grid=(nb, m // tm),
        in_specs=[act(A_WIDTH), act(B_WIDTH), act(C_WIDTH), wgt(A_WIDTH), wgt(B_WIDTH), wgt(C_WIDTH),
                  gate(0), gate(1), gate(2)],
        out_specs=pl.BlockSpec((tm, tn), lambda j, i: (i, j)),
        compiler_params=_cparams(("parallel", "parallel")),
        name="branch_merge",
    )(ya, yb, yc, w_a, w_b, w_c, gates, gates, gates)


def kernel(x_prompt, x_sample, cache_k, cache_v, state_hgrn, state_conv, page_table, w_in, b_gate,
           conv_w, hgrn_lb_logits, hgrn_norm_g, w_a_up, w_b_up, w_c_up, w_o, rel_bias, g_pre_mix,
           g_post_mix, g_pre_mlp, g_post_mlp, w_mlp_up, w_mlp_down):
    consts = {k: jnp.asarray(v) for k, v in _constants().items()}
    for name in ("m3_p", "emask_p", "m3_s", "emask_s"):
        consts[name] = consts[name].astype(BF16)

    w_in_b, w_a_b, w_b_b, w_c_b = (w.astype(BF16) for w in (w_in, w_a_up, w_b_up, w_c_up))
    w_o_b, w_up_b, w_dn_b = (w.astype(BF16) for w in (w_o, w_mlp_up, w_mlp_down))
    vec = lambda g: g.reshape(DEPTH, 1, g.shape[-1])
    b_gate3, ng3 = vec(b_gate), vec(hgrn_norm_g)
    gpm, gqm, gpl, gql = vec(g_pre_mix), vec(g_post_mix), vec(g_pre_mlp), vec(g_post_mlp)
    pt_flat = page_table.reshape(-1)

    lbp = _lower_bound_params(hgrn_lb_logits)
    bown, bprev, bias_s = _bias_tables(rel_bias, consts)
    means = _cache_block_means(cache_k, pt_flat)

    groups = {
        "p": dict(x=x_prompt.reshape(N_TOK_P, D_MODEL), tm_row=128, mm=dict(tm=1024, tn=1024, tk=1024),
                  merge=dict(tm=512, tn=1024)),
        "s": dict(x=x_sample.reshape(N_TOK_S, D_MODEL), tm_row=N_TOK_S, mm=dict(tm=N_TOK_S, tn=1024, tk=4096),
                  merge=dict(tm=N_TOK_S, tn=1024)),
    }
    for g in groups.values():
        g["h"] = _prenorm(g["x"], gpm, 0, tm=g["tm_row"])
    outs = {k: [] for k in ("kp", "vp", "ks", "vs", "sp", "ss", "cp", "cs")}

    for l in range(DEPTH):
        for tag, g in groups.items():
            mm = g["mm"]
            proj = _matmul(g["h"], w_in_b, l, col0=0, n_cols=N_MAIN, name="in_proj", **mm)
            gates = _matmul(g["h"], w_in_b, l, col0=N_MAIN, n_cols=N_GATE, epilogue="sigmoid_bias",
                            bias=b_gate3, name="in_proj_gates", **mm)
            if tag == "p":
                ya, st = _hgrn_prompt(proj, lbp, ng3, l, consts)
                yb, cn = _shortconv(proj.reshape(BATCH, SEQ, N_MAIN), conv_w, l, None, tc=256, out_dtype=BF16)
                yb = yb.reshape(N_TOK_P, B_WIDTH)
                yc = _moba_prompt(proj, rel_bias, bown, bprev)
                kv_shape = (BATCH, SEQ, C_HEADS, C_DIM)
            else:
                ya, st = _hgrn_sample(proj, state_hgrn, lbp, ng3, l, consts)
                yb, cn = _shortconv(proj.reshape(DEC_BATCH, DEC_SEQ, N_MAIN), conv_w, l, state_conv,
                                    tc=B_WIDTH, out_dtype=F32)
                yb = yb.reshape(N_TOK_S, B_WIDTH).astype(BF16)
                idx = _moba_gate(proj, means, l)
                idx_flat = idx[:, :, :MOBA_TOPK].reshape(-1)
                yc = _moba_sample(proj, cache_k, cache_v, idx_flat, pt_flat, rel_bias, bias_s, l).astype(BF16)
                kv_shape = (DEC_BATCH, DEC_SEQ, C_HEADS, C_DIM)
            outs["k" + tag].append(proj[:, OFF_CK * 128:OFF_CK * 128 + C_WIDTH].reshape(kv_shape))
            outs["v" + tag].append(proj[:, OFF_CV * 128:OFF_CV * 128 + C_WIDTH].reshape(kv_shape))
            outs["s" + tag].append(st)
            outs["c" + tag].append(cn)

            merged = _merge(ya, yb, yc, gates, w_a_b, w_b_b, w_c_b, l, **g["merge"])
            mix = _matmul(merged, w_o_b, l, name="out_proj", **mm)
            x, hm = _post(g["x"], mix, gqm, l, gpl, l, tm=g["tm_row"])
            up = _matmul(hm, w_up_b, l, epilogue="relu2", out_dtype=BF16, name="mlp_up", **mm)
            dn = _matmul(up, w_dn_b, l, name="mlp_down", **mm)
            if l + 1 < DEPTH:
                g["x"], g["h"] = _post(x, dn, gql, l, gpm, l + 1, tm=g["tm_row"])
            else:
                g["x"], g["h"] = _post(x, dn, gql, l, tm=g["tm_row"])

    return (groups["p"]["x"].reshape(BATCH, SEQ, D_MODEL),
            groups["s"]["x"].reshape(DEC_BATCH, DEC_SEQ, D_MODEL),
            jnp.stack(outs["kp"]), jnp.stack(outs["vp"]), jnp.stack(outs["ks"]), jnp.stack(outs["vs"]),
            jnp.stack(outs["sp"]), jnp.stack(outs["ss"]), jnp.stack(outs["cp"]), jnp.stack(outs["cs"]))
```

```python
import functools
import math

import numpy as np
import jax
import jax.numpy as jnp
from jax import lax
from jax.experimental import pallas as pl
from jax.experimental.pallas import tpu as pltpu

F32 = jnp.float32
BF16 = jnp.bfloat16

D_MODEL = 4096
BATCH = 4
SEQ = 2048
DEPTH = 4
DEC_BATCH = 8
DEC_SEQ = 4
PAST_LEN = 8192
PAGE_SIZE = 128
A_HEADS = 8
A_DIM = 128
A_WIDTH = A_HEADS * A_DIM
A_CHUNK = 16
B_WIDTH = D_MODEL // 4
C_HEADS = 16
C_DIM = 128
C_WIDTH = C_HEADS * C_DIM
MOBA_BLOCK = 256
MOBA_TOPK = 3
REL_BUCKETS = 32
REL_MAX_DIST = 128
D_FF = 4 * D_MODEL
NORM_EPS = 1e-6
NEG_BIG = -1e30
LB_FLOOR = 1e-30

N_TOK_P = BATCH * SEQ
N_TOK_S = DEC_BATCH * DEC_SEQ
N_MAIN = 4 * A_WIDTH + 3 * B_WIDTH + 3 * C_WIDTH
N_GATE = 3 * D_MODEL
OFF_AQ, OFF_AF, OFF_AI, OFF_AG = 0, 8, 16, 24
OFF_BB, OFF_BC, OFF_BX = 32, 40, 48
OFF_CQ, OFF_CK, OFF_CV = 56, 72, 88
N_PAGES = PAST_LEN // PAGE_SIZE
N_PAST_BLK = PAST_LEN // MOBA_BLOCK
PAGES_PER_BLK = MOBA_BLOCK // PAGE_SIZE
FRAME = 128

V7X_VMEM_LIMIT = 48 * 1024 * 1024
V7X_VMEM_BIG = 56 * 1024 * 1024

_NT = (((1,), (1,)), ((), ()))


def _cparams(sem, vmem=V7X_VMEM_LIMIT):
    return pltpu.CompilerParams(dimension_semantics=sem, vmem_limit_bytes=vmem)


def _sigmoid(x):
    return 1.0 / (1.0 + jnp.exp(-x))


def _split3(x):
    hi = x.astype(BF16)
    r1 = x - hi.astype(F32)
    lo = r1.astype(BF16)
    lo2 = (r1 - lo.astype(F32)).astype(BF16)
    return hi, lo, lo2


def _split2(x):
    hi = x.astype(BF16)
    lo = (x - hi.astype(F32)).astype(BF16)
    return hi, lo


def _rel_bucket_np(rel):
    n = np.maximum(rel, 0)
    max_exact = REL_BUCKETS // 2
    nf = np.maximum(n, max_exact).astype(np.float32)
    large = max_exact + (np.log(nf / max_exact) / math.log(REL_MAX_DIST / max_exact)
                         * (REL_BUCKETS - max_exact)).astype(np.int32)
    large = np.clip(large, max_exact, REL_BUCKETS - 1)
    return np.where(n < max_exact, n, large).astype(np.int32)


def _chunk_mats(n, chunk):
    r = np.arange(n)[:, None]
    c = np.arange(n)[None, :]
    same = (r // chunk) == (c // chunk)
    t_cum = same & (c <= r)
    t_last = same
    t_ref = same & ((c % chunk) <= chunk // 2)
    m3 = np.concatenate([t_cum, t_last, t_ref], axis=0).astype(np.float32)
    return m3, t_cum.astype(np.float32)


@functools.lru_cache(maxsize=None)
def _constants():
    m3_p, causal_p = _chunk_mats(MOBA_BLOCK, A_CHUNK)
    rows = np.arange(16 * A_DIM)[:, None]
    cols = np.arange(256)[None, :]
    emask_p = ((rows // A_DIM) == (cols // A_CHUNK)).astype(np.float32)
    m3_s, causal_s = _chunk_mats(FRAME, DEC_SEQ)
    rows = np.arange(DEC_BATCH * A_DIM)[:, None]
    cols = np.arange(FRAME)[None, :]
    emask_s = (((rows // A_DIM) == (cols // DEC_SEQ)) & (cols < N_TOK_S)).astype(np.float32)
    kl = np.arange(MOBA_BLOCK)[:, None]
    ql = np.arange(MOBA_BLOCK)[None, :]
    bucket_own = _rel_bucket_np(ql - kl)
    bucket_prev = _rel_bucket_np(MOBA_BLOCK + ql - kl)
    s = np.arange(8)[:, None]
    t = np.arange(MOBA_BLOCK)[None, :]
    bucket_s = _rel_bucket_np(MOBA_BLOCK + s - t)
    return dict(m3_p=m3_p, causal_p=causal_p, emask_p=emask_p, m3_s=m3_s, causal_s=causal_s,
                emask_s=emask_s, bucket_own=bucket_own, bucket_prev=bucket_prev, bucket_s=bucket_s)


def _mm_body(*refs, nk, epilogue):
    if epilogue == "sigmoid_bias":
        a_ref, w_ref, b_ref, o_ref = refs
    else:
        a_ref, w_ref, o_ref = refs
        b_ref = None

    def finish(r):
        if epilogue == "sigmoid_bias":
            r = _sigmoid(r + b_ref[...])
        elif epilogue == "relu2":
            r = jnp.square(jnp.maximum(r, 0.0))
        o_ref[...] = r.astype(o_ref.dtype)

    if nk == 1:
        finish(jnp.dot(a_ref[...], w_ref[...], preferred_element_type=F32))
        return

    @pl.when(pl.program_id(2) == 0)
    def _():
        o_ref[...] = jnp.zeros(o_ref.shape, F32)

    o_ref[...] += jnp.dot(a_ref[...], w_ref[...], preferred_element_type=F32)


def _matmul(a, w, layer, *, col0=0, n_cols=None, tm, tn, tk, epilogue="none", bias=None,
            out_dtype=F32, name):
    m, kdim = a.shape
    n_cols = w.shape[2] if n_cols is None else n_cols
    assert m % tm == 0 and kdim % tk == 0 and n_cols % tn == 0 and col0 % tn == 0
    nk = kdim // tk
    assert nk == 1 or (epilogue == "none" and out_dtype == F32)
    cb = col0 // tn
    in_specs = [pl.BlockSpec((tm, tk), lambda i, j, k: (i, k)),
                pl.BlockSpec((None, tk, tn), lambda i, j, k: (layer, k, j + cb))]
    args = [a, w]
    if epilogue == "sigmoid_bias":
        in_specs.append(pl.BlockSpec((None, 1, tn), lambda i, j, k: (layer, 0, j)))
        args.append(bias)
    return pl.pallas_call(
        functools.partial(_mm_body, nk=nk, epilogue=epilogue),
        out_shape=jax.ShapeDtypeStruct((m, n_cols), out_dtype),
        grid=(m // tm, n_cols // tn, nk),
        in_specs=in_specs,
        out_specs=pl.BlockSpec((tm, tn), lambda i, j, k: (i, j)),
        compiler_params=_cparams(("parallel", "parallel", "arbitrary"), V7X_VMEM_BIG),
        name=name,
    )(*args)


def _rms(x, g):
    return x * lax.rsqrt(jnp.mean(x * x, axis=-1, keepdims=True) + NORM_EPS) * g


def _prenorm_body(x_ref, g_ref, h_ref):
    h_ref[...] = _rms(x_ref[...], g_ref[...]).astype(h_ref.dtype)


def _prenorm(x, g, layer, *, tm):
    m = x.shape[0]
    return pl.pallas_call(
        _prenorm_body,
        out_shape=jax.ShapeDtypeStruct((m, D_MODEL), BF16),
        grid=(m // tm,),
        in_specs=[pl.BlockSpec((tm, D_MODEL), lambda i: (i, 0)),
                  pl.BlockSpec((None, 1, D_MODEL), lambda i: (layer, 0, 0))],
        out_specs=pl.BlockSpec((tm, D_MODEL), lambda i: (i, 0)),
        compiler_params=_cparams(("parallel",)),
        name="prenorm",
    )(x, g)


def _post_body(x_ref, y_ref, gpost_ref, *rest, with_next):
    xn = x_ref[...] + _rms(y_ref[...], gpost_ref[...])
    if with_next:
        gnext_ref, xo_ref, ho_ref = rest
        ho_ref[...] = _rms(xn, gnext_ref[...]).astype(ho_ref.dtype)
    else:
        (xo_ref,) = rest
    xo_ref[...] = xn


def _post(x, y, gpost, layer, gnext=None, next_layer=None, *, tm):
    m = x.shape[0]
    with_next = gnext is not None
    row = pl.BlockSpec((tm, D_MODEL), lambda i: (i, 0))
    in_specs = [row, row, pl.BlockSpec((None, 1, D_MODEL), lambda i: (layer, 0, 0))]
    args = [x, y, gpost]
    out_shape = [jax.ShapeDtypeStruct((m, D_MODEL), F32)]
    out_specs = [row]
    if with_next:
        in_specs.append(pl.BlockSpec((None, 1, D_MODEL), lambda i: (next_layer, 0, 0)))
        args.append(gnext)
        out_shape.append(jax.ShapeDtypeStruct((m, D_MODEL), BF16))
        out_specs.append(row)
    res = pl.pallas_call(
        functools.partial(_post_body, with_next=with_next),
        out_shape=out_shape,
        grid=(m // tm,),
        in_specs=in_specs,
        out_specs=out_specs,
        compiler_params=_cparams(("parallel",)),
        name="post_norm",
    )(*args)
    return (res[0], res[1]) if with_next else (res[0], None)


def _lb_body(x_ref, o_ref):
    x = x_ref[...]
    rows = [x[l:l + 1] for l in range(DEPTH)]
    mx = functools.reduce(jnp.maximum, rows)
    es = [jnp.exp(r - mx) for r in rows]
    tot = functools.reduce(lambda a, b: a + b, es)
    ps = [e / tot for e in es]
    cum = None
    for l in range(DEPTH):
        cum = ps[l] if cum is None else cum + ps[l]
        lb = jnp.clip(cum - ps[0], 0.0, 1.0)
        o_ref[3 * l:3 * l + 1, :] = jnp.log(jnp.maximum(lb, LB_FLOOR))
        o_ref[3 * l + 1:3 * l + 2, :] = jnp.log1p(-lb)
        o_ref[3 * l + 2:3 * l + 3, :] = 1.0 - lb


def _lower_bound_params(lb_logits):
    out = pl.pallas_call(
        _lb_body,
        out_shape=jax.ShapeDtypeStruct((3 * DEPTH, A_WIDTH), F32),
        name="hgrn_lower_bounds",
    )(lb_logits)
    return out.reshape(DEPTH, 3, A_HEADS, 1, A_DIM)


def _hgrn_gates(aq, z, la, l1m, oml):
    q = aq * _sigmoid(aq)
    log_sig = jnp.minimum(z, 0.0) - jnp.log1p(jnp.exp(-jnp.abs(z)))
    b = l1m + log_sig
    log_f = jnp.maximum(la, b) + jnp.log1p(jnp.exp(-jnp.abs(la - b)))
    k = oml / (1.0 + jnp.exp(z))
    return q, k, log_f


def _chunk_sums(m3, log_f, n):
    hi, lo, lo2 = _split3(log_f)
    c3 = (jnp.dot(m3, hi, preferred_element_type=F32)
          + jnp.dot(m3, lo, preferred_element_type=F32)
          + jnp.dot(m3, lo2, preferred_element_type=F32))
    return c3[0:n], c3[n:2 * n], c3[2 * n:3 * n]


def _hgrn_prompt_body(aq_ref, af_ref, ai_ref, ag_ref, lbp_ref, ng_ref, m3_ref, cmask_ref, emask_ref,
                      ya_ref, st_ref, o_scr, qc_scr, dec_scr, ut_scr):
    blk = MOBA_BLOCK
    n_blk = SEQ // blk
    cpb = blk // A_CHUNK
    la = lbp_ref[0]
    l1m = lbp_ref[1]
    oml = lbp_ref[2]

    def block_body(j, carry):
        rows = pl.ds(pl.multiple_of(j * blk, blk), blk)
        q, k, log_f = _hgrn_gates(aq_ref[rows, :], af_ref[rows, :], la, l1m, oml)
        v = ai_ref[rows, :]
        cum, last_b, ref_b = _chunk_sums(m3_ref[...], log_f, blk)
        q_rel = (q * jnp.exp(cum - ref_b)).astype(BF16)
        k_rel = (k * jnp.exp(ref_b - cum)).astype(BF16)
        k_last = (k * jnp.exp(last_b - cum)).astype(BF16)
        a = lax.dot_general(q_rel, k_rel, _NT, preferred_element_type=F32)
        a = jnp.where(cmask_ref[...] > 0.0, a, 0.0)
        o_scr[rows, :] = jnp.dot(a.astype(BF16), v.astype(BF16), preferred_element_type=F32)
        qc_scr[rows, :] = (q * jnp.exp(cum)).astype(BF16)
        dec_scr[rows, :] = jnp.exp(last_b)
        vt = v.T.astype(BF16)
        vt_exp = jnp.concatenate([vt] * cpb, axis=0) * emask_ref[...]
        ut_rows = pl.ds(pl.multiple_of(j * (cpb * A_DIM), cpb * A_DIM), cpb * A_DIM)
        ut_scr[ut_rows, :] = jnp.dot(vt_exp, k_last, preferred_element_type=F32)
        return carry

    lax.fori_loop(0, n_blk, block_body, 0)

    def scan_body(n, st):
        rows = pl.ds(pl.multiple_of(n * A_DIM, A_DIM), A_DIM)
        u = ut_scr[rows, :]
        ut_scr[rows, :] = st
        d = dec_scr[pl.ds(pl.multiple_of(n * A_CHUNK, A_CHUNK), 8), :][0:1, :]
        return st * d + u

    st = lax.fori_loop(0, SEQ // A_CHUNK, scan_body, jnp.zeros((A_DIM, A_DIM), F32))
    st_ref[...] = st.T

    ng = ng_ref[...]

    def out_body(j, carry):
        rows = pl.ds(pl.multiple_of(j * blk, blk), blk)
        ut_rows = pl.ds(pl.multiple_of(j * (cpb * A_DIM), cpb * A_DIM), cpb * A_DIM)
        starts = ut_scr[ut_rows, :].astype(BF16)
        r = lax.dot_general(qc_scr[rows, :], starts, _NT, preferred_element_type=F32)
        o_inter = jnp.concatenate(
            [r[n * A_CHUNK:(n + 1) * A_CHUNK, n * A_DIM:(n + 1) * A_DIM] for n in range(cpb)], axis=0)
        o = _rms(o_scr[rows, :] + o_inter, ng)
        ag = ag_ref[rows, :]
        ya_ref[rows, :] = (o * (ag * _sigmoid(ag))).astype(ya_ref.dtype)
        return carry

    lax.fori_loop(0, n_blk, out_body, 0)


def _hgrn_prompt(proj, lbp, norm_g, layer, consts):
    col = lambda off: pl.BlockSpec((SEQ, A_DIM), lambda b, h: (b, off + h))
    whole = lambda shape: pl.BlockSpec(shape, lambda b, h: (0,) * len(shape))
    return pl.pallas_call(
        _hgrn_prompt_body,
        out_shape=[jax.ShapeDtypeStruct((N_TOK_P, A_WIDTH), BF16),
                   jax.ShapeDtypeStruct((BATCH, A_HEADS, A_DIM, A_DIM), F32)],
        grid=(BATCH, A_HEADS),
        in_specs=[col(OFF_AQ), col(OFF_AF), col(OFF_AI), col(OFF_AG),
                  pl.BlockSpec((None, 3, None, 1, A_DIM), lambda b, h: (layer, 0, h, 0, 0)),
                  pl.BlockSpec((None, 1, A_DIM), lambda b, h: (layer, 0, 0)),
                  whole((3 * MOBA_BLOCK, MOBA_BLOCK)), whole((MOBA_BLOCK, MOBA_BLOCK)),
                  whole((16 * A_DIM, MOBA_BLOCK))],
        out_specs=[pl.BlockSpec((SEQ, A_DIM), lambda b, h: (b, h)),
                   pl.BlockSpec((None, None, A_DIM, A_DIM), lambda b, h: (b, h, 0, 0))],
        scratch_shapes=[pltpu.VMEM((SEQ, A_DIM), F32), pltpu.VMEM((SEQ, A_DIM), BF16),
                        pltpu.VMEM((SEQ, A_DIM), F32), pltpu.VMEM((SEQ // A_CHUNK * A_DIM, A_DIM), F32)],
        compiler_params=_cparams(("parallel", "parallel")),
        name="hgrn_prompt",
    )(proj, proj, proj, proj, lbp, norm_g, consts["m3_p"], consts["causal_p"], consts["emask_p"])


def _pad_frame(x):
    return jnp.concatenate([x, jnp.zeros((FRAME - x.shape[0], x.shape[1]), x.dtype)], axis=0)


def _hgrn_sample_body(aq_ref, af_ref, ai_ref, ag_ref, s0_ref, lbp_ref, ng_ref, m3_ref, cmask_ref,
                      emask_ref, ya_ref, sn_ref):
    la = lbp_ref[0]
    l1m = lbp_ref[1]
    oml = lbp_ref[2]
    q, k, log_f = _hgrn_gates(_pad_frame(aq_ref[...]), _pad_frame(af_ref[...]), la, l1m, oml)
    v = _pad_frame(ai_ref[...])
    cum, last_b, ref_b = _chunk_sums(m3_ref[...], log_f, FRAME)
    q_rel = (q * jnp.exp(cum - ref_b)).astype(BF16)
    k_rel = (k * jnp.exp(ref_b - cum)).astype(BF16)
    k_last = (k * jnp.exp(last_b - cum)).astype(BF16)
    a = lax.dot_general(q_rel, k_rel, _NT, preferred_element_type=F32)
    a = jnp.where(cmask_ref[...] > 0.0, a, 0.0)
    o_intra = jnp.dot(a.astype(BF16), v.astype(BF16), preferred_element_type=F32)
    qc = (q * jnp.exp(cum)).astype(BF16)
    dec = jnp.exp(last_b)
    vt = v.T.astype(BF16)
    vt_exp = jnp.concatenate([vt] * DEC_BATCH, axis=0) * emask_ref[...]
    ut = jnp.dot(vt_exp, k_last, preferred_element_type=F32)
    s0t = jnp.concatenate([s0_ref[b].T for b in range(DEC_BATCH)], axis=0)
    r = lax.dot_general(qc, s0t.astype(BF16), _NT, preferred_element_type=F32)
    row_b = lax.broadcasted_iota(jnp.int32, (FRAME, A_DIM), 0) >> int(math.log2(DEC_SEQ))
    o_inter = jnp.zeros((FRAME, A_DIM), F32)
    for b in range(DEC_BATCH):
        o_inter = jnp.where(row_b == b, r[:, b * A_DIM:(b + 1) * A_DIM], o_inter)
        d = dec[b * DEC_SEQ:b * DEC_SEQ + 1, :]
        snt = s0t[b * A_DIM:(b + 1) * A_DIM, :] * d + ut[b * A_DIM:(b + 1) * A_DIM, :]
        sn_ref[b] = snt.T
    o = _rms(o_intra + o_inter, ng_ref[...])
    ag = _pad_frame(ag_ref[...])
    ya_ref[...] = (o * (ag * _sigmoid(ag)))[0:N_TOK_S].astype(ya_ref.dtype)


def _hgrn_sample(proj, state, lbp, norm_g, layer, consts):
    col = lambda off: pl.BlockSpec((N_TOK_S, A_DIM), lambda h: (0, off + h))
    whole = lambda shape: pl.BlockSpec(shape, lambda h: (0,) * len(shape))
    return pl.pallas_call(
        _hgrn_sample_body,
        out_shape=[jax.ShapeDtypeStruct((N_TOK_S, A_WIDTH), BF16),
                   jax.ShapeDtypeStruct((DEC_BATCH, A_HEADS, A_DIM, A_DIM), F32)],
        grid=(A_HEADS,),
        in_specs=[col(OFF_AQ), col(OFF_AF), col(OFF_AI), col(OFF_AG),
                  pl.BlockSpec((None, DEC_BATCH, None, A_DIM, A_DIM), lambda h: (layer, 0, h, 0, 0)),
                  pl.BlockSpec((None, 3, None, 1, A_DIM), lambda h: (layer, 0, h, 0, 0)),
                  pl.BlockSpec((None, 1, A_DIM), lambda h: (layer, 0, 0)),
                  whole((3 * FRAME, FRAME)), whole((FRAME, FRAME)), whole((DEC_BATCH * A_DIM, FRAME))],
        out_specs=[pl.BlockSpec((N_TOK_S, A_DIM), lambda h: (0, h)),
                   pl.BlockSpec((DEC_BATCH, None, A_DIM, A_DIM), lambda h: (0, h, 0, 0))],
        compiler_params=_cparams(("parallel",)),
        name="hgrn_sample",
    )(proj, proj, proj, proj, state, lbp, norm_g, consts["m3_s"], consts["causal_s"], consts["emask_s"])


def _conv_body(bb_ref, bc_ref, bx_ref, w_ref, *rest, t, has_prev):
    if has_prev:
        prev_ref, y_ref, cn_ref, scr = rest
    else:
        y_ref, cn_ref, scr = rest
    u = bc_ref[...] * bx_ref[...]
    scr[0:8, :] = jnp.zeros((8, scr.shape[1]), F32)
    if has_prev:
        scr[6:8, :] = prev_ref[...]
    scr[8:8 + t, :] = u
    w = w_ref[...]
    y = w[0:1] * scr[6:6 + t, :] + w[1:2] * scr[7:7 + t, :] + w[2:3] * u
    y_ref[...] = (bb_ref[...] * y).astype(y_ref.dtype)
    cn_ref[...] = scr[6 + t:8 + t, :]


def _shortconv(proj3, conv_w, layer, prev, *, tc, out_dtype):
    bsz, t, _ = proj3.shape
    has_prev = prev is not None
    lanes = lambda off: pl.BlockSpec((None, t, tc), lambda b, c: (b, 0, off * 128 // tc + c))
    in_specs = [lanes(OFF_BB), lanes(OFF_BC), lanes(OFF_BX),
                pl.BlockSpec((None, 3, tc), lambda b, c: (layer, 0, c))]
    args = [proj3, proj3, proj3, conv_w]
    if has_prev:
        in_specs.append(pl.BlockSpec((None, None, 2, tc), lambda b, c: (layer, b, 0, c)))
        args.append(prev)
    return pl.pallas_call(
        functools.partial(_conv_body, t=t, has_prev=has_prev),
        out_shape=[jax.ShapeDtypeStruct((bsz, t, B_WIDTH), out_dtype),
                   jax.ShapeDtypeStruct((bsz, 2, B_WIDTH), F32)],
        grid=(bsz, B_WIDTH // tc),
        in_specs=in_specs,
        out_specs=[pl.BlockSpec((None, t, tc), lambda b, c: (b, 0, c)),
                   pl.BlockSpec((None, 2, tc), lambda b, c: (b, 0, c))],
        scratch_shapes=[pltpu.VMEM((t + 8, tc), F32)],
        compiler_params=_cparams(("parallel", "parallel")),
        name="shortconv",
    )(*args)


def _bias_body(rb_ref, bo_ref, bp_ref, bs_ref, own_ref, prev_ref, smp_ref):
    h = pl.program_id(0)

    def lookup(bucket):
        acc = jnp.zeros(bucket.shape, F32)
        for b in range(REL_BUCKETS):
            acc = jnp.where(bucket == b, rb_ref[b, h], acc)
        return acc

    blk = MOBA_BLOCK
    key_i = lax.broadcasted_iota(jnp.int32, (blk, blk), 0)
    qry_i = lax.broadcasted_iota(jnp.int32, (blk, blk), 1)
    own_ref[...] = jnp.where(key_i <= qry_i, lookup(bo_ref[...]), NEG_BIG)
    prev_ref[...] = lookup(bp_ref[...])
    smp_ref[...] = lookup(bs_ref[...])


def _bias_tables(rel_bias, consts):
    whole = lambda shape: pl.BlockSpec(shape, lambda h: (0,) * len(shape))
    blk = MOBA_BLOCK
    return pl.pallas_call(
        _bias_body,
        out_shape=[jax.ShapeDtypeStruct((C_HEADS, blk, blk), F32),
                   jax.ShapeDtypeStruct((C_HEADS, blk, blk), F32),
                   jax.ShapeDtypeStruct((C_HEADS, 8, blk), F32)],
        grid=(C_HEADS,),
        in_specs=[pl.BlockSpec(memory_space=pltpu.SMEM),
                  whole((blk, blk)), whole((blk, blk)), whole((8, blk))],
        out_specs=[pl.BlockSpec((None, blk, blk), lambda h: (h, 0, 0)),
                   pl.BlockSpec((None, blk, blk), lambda h: (h, 0, 0)),
                   pl.BlockSpec((None, 8, blk), lambda h: (h, 0, 0))],
        compiler_params=_cparams(("parallel",)),
        name="rel_bias_tables",
    )(rel_bias, consts["bucket_own"], consts["bucket_prev"], consts["bucket_s"])


def _moba_prompt_body(rb_ref, q_ref, k_ref, v_ref, bown_ref, bprev_ref, o_ref, kbf, vtbf):
    blk = MOBA_BLOCK
    n_blk = SEQ // blk
    h = pl.program_id(1)
    scale = C_DIM ** -0.5
    far_bias = rb_ref[REL_BUCKETS - 1, h]

    means = []
    for i in range(n_blk):
        kb = k_ref[i * blk:(i + 1) * blk, :]
        kbf[i * blk:(i + 1) * blk, :] = kb.astype(BF16)
        means.append(jnp.sum(kb, axis=0, keepdims=True) / blk)
        vtbf[:, i * blk:(i + 1) * blk] = v_ref[i * blk:(i + 1) * blk, :].T.astype(BF16)
    means = jnp.concatenate(means + [jnp.zeros((n_blk, C_DIM), F32)], axis=0)
    mh, ml = _split2(means)
    blk_i = lax.broadcasted_iota(jnp.int32, (n_blk, blk), 0)

    for qi in range(n_blk):
        q = q_ref[qi * blk:(qi + 1) * blk, :]
        n_keys = (qi + 1) * blk
        s = lax.dot_general(kbf[0:n_keys, :], q.astype(BF16), _NT, preferred_element_type=F32) * scale
        if qi >= 1:
            qh, ql = _split2(q)
            gate = (lax.dot_general(mh, qh, _NT, preferred_element_type=F32)
                    + lax.dot_general(mh, ql, _NT, preferred_element_type=F32)
                    + lax.dot_general(ml, qh, _NT, preferred_element_type=F32))[0:n_blk]
            rank = jnp.zeros((n_blk, blk), jnp.int32)
            for jp in range(qi):
                gj = gate[jp:jp + 1, :]
                beats = (gj > gate) | ((gj == gate) & (jp < blk_i))
                rank = rank + jnp.where(beats, 1, 0)
            keep = (blk_i < qi) & (rank < MOBA_TOPK)
            far_row = jnp.where(keep, far_bias, NEG_BIG)
            prev_row = jnp.where(keep, 0.0, NEG_BIG)
        pieces = []
        for j in range(qi + 1):
            sj = s[j * blk:(j + 1) * blk]
            if j == qi:
                sj = sj + bown_ref[...]
            elif j == qi - 1:
                sj = sj + bprev_ref[...] + prev_row[j:j + 1]
            else:
                sj = sj + far_row[j:j + 1]
            pieces.append(sj)
        t = jnp.concatenate(pieces, axis=0) if qi else pieces[0]
        m = jnp.max(t, axis=0, keepdims=True)
        p = jnp.exp(t - m)
        l = jnp.sum(p, axis=0, keepdims=True)
        acc = jnp.dot(vtbf[:, 0:n_keys], p.astype(BF16), preferred_element_type=F32)
        o_ref[qi * blk:(qi + 1) * blk, :] = (acc / l).T.astype(o_ref.dtype)


def _moba_prompt(proj, rel_bias, bown, bprev):
    blk = MOBA_BLOCK
    seq_col = lambda off: pl.BlockSpec((SEQ, C_DIM), lambda b, h: (b, off + h))
    return pl.pallas_call(
        _moba_prompt_body,
        out_shape=jax.ShapeDtypeStruct((N_TOK_P, C_WIDTH), BF16),
        grid=(BATCH, C_HEADS),
        in_specs=[pl.BlockSpec(memory_space=pltpu.SMEM),
                  seq_col(OFF_CQ), seq_col(OFF_CK), seq_col(OFF_CV),
                  pl.BlockSpec((None, blk, blk), lambda b, h: (h, 0, 0)),
                  pl.BlockSpec((None, blk, blk), lambda b, h: (h, 0, 0))],
        out_specs=pl.BlockSpec((SEQ, C_DIM), lambda b, h: (b, h)),
        scratch_shapes=[pltpu.VMEM((SEQ, C_DIM), BF16), pltpu.VMEM((C_DIM, SEQ), BF16)],
        compiler_params=_cparams(("parallel", "parallel")),
        name="moba_prompt",
    )(rel_bias, proj, proj, proj, bown, bprev)


def _cache_means_body(pt_ref, k0_ref, k1_ref, o_ref):
    o_ref[...] = (jnp.sum(k0_ref[...], axis=0) + jnp.sum(k1_ref[...], axis=0)) / MOBA_BLOCK


def _cache_block_means(cache_k, pt_flat):
    def page(j):
        return pl.BlockSpec((None, None, PAGE_SIZE, C_HEADS, C_DIM),
                            lambda l, b, n, pt: (l, pt[b * N_PAGES + n * PAGES_PER_BLK + j], 0, 0, 0))
    return pl.pallas_call(
        _cache_means_body,
        out_shape=jax.ShapeDtypeStruct((DEPTH, DEC_BATCH, N_PAST_BLK, C_HEADS, C_DIM), F32),
        grid_spec=pltpu.PrefetchScalarGridSpec(
            num_scalar_prefetch=1,
            grid=(DEPTH, DEC_BATCH, N_PAST_BLK),
            in_specs=[page(0), page(1)],
            out_specs=pl.BlockSpec((None, None, None, C_HEADS, C_DIM),
                                   lambda l, b, n, pt: (l, b, n, 0, 0))),
        compiler_params=_cparams(("parallel", "parallel", "parallel")),
        name="cache_block_means",
    )(pt_flat, cache_k, cache_k)


def _moba_gate_body(q_ref, means_ref, idx_ref):
    n_cand = DEC_BATCH * N_PAST_BLK
    q = q_ref[...]
    means = means_ref[...].reshape(n_cand, C_DIM)
    qh, ql = _split2(q)
    mh, ml = _split2(means)
    g = (lax.dot_general(qh, mh, _NT, preferred_element_type=F32)
         + lax.dot_general(qh, ml, _NT, preferred_element_type=F32)
         + lax.dot_general(ql, mh, _NT, preferred_element_type=F32))
    col = lax.broadcasted_iota(jnp.int32, (N_TOK_S, n_cand), 1)
    row = lax.broadcasted_iota(jnp.int32, (N_TOK_S, n_cand), 0)
    own_seq = (col >> int(math.log2(N_PAST_BLK))) == (row >> int(math.log2(DEC_SEQ)))
    g = jnp.where(own_seq, g, -jnp.inf)
    colf = col.astype(F32)
    lane = lax.broadcasted_iota(jnp.int32, (N_TOK_S, 128), 1)
    out = jnp.zeros((N_TOK_S, 128), jnp.int32)
    for r in range(MOBA_TOPK):
        mx = jnp.max(g, axis=1, keepdims=True)
        am = jnp.min(jnp.where(g == mx, colf, float(n_cand)), axis=1, keepdims=True)
        out = jnp.where(lane == r, am.astype(jnp.int32) & (N_PAST_BLK - 1), out)
        g = jnp.where(colf == am, -jnp.inf, g)
    idx_ref[...] = out


def _moba_gate(proj_s, means, layer):
    return pl.pallas_call(
        _moba_gate_body,
        out_shape=jax.ShapeDtypeStruct((C_HEADS, N_TOK_S, 128), jnp.int32),
        grid=(C_HEADS,),
        in_specs=[pl.BlockSpec((N_TOK_S, C_DIM), lambda h: (0, OFF_CQ + h)),
                  pl.BlockSpec((None, DEC_BATCH, N_PAST_BLK, C_DIM), lambda h: (layer, 0, 0, h))],
        out_specs=pl.BlockSpec((None, N_TOK_S, 128), lambda h: (h, 0, 0)),
        compiler_params=_cparams(("parallel",)),
        name="moba_gate",
    )(proj_s, means.reshape(DEPTH, DEC_BATCH, N_PAST_BLK, C_WIDTH))


def _moba_sample_body(idx_ref, pt_ref, rb_ref, q_ref, kn_ref, vn_ref, bs_ref, ck_ref, cv_ref, o_ref,
                      kbuf, vbuf, sem, *, layer):
    h = pl.program_id(0)
    b = pl.program_id(1)
    step = h * DEC_BATCH + b
    n_steps = C_HEADS * DEC_BATCH
    scale = C_DIM ** -0.5

    def block_of(hh, bb, s, r):
        return idx_ref[(hh * N_TOK_S + bb * DEC_SEQ + s) * MOBA_TOPK + r]

    def page_copies(st, slot):
        hh = st >> int(math.log2(DEC_BATCH))
        bb = st & (DEC_BATCH - 1)
        copies = []
        for s in range(DEC_SEQ):
            for r in range(MOBA_TOPK):
                n = block_of(hh, bb, s, r)
                for j in range(PAGES_PER_BLK):
                    pid = pt_ref[bb * N_PAGES + n * PAGES_PER_BLK + j]
                    i = (s * MOBA_TOPK + r) * PAGES_PER_BLK + j
                    copies.append(pltpu.make_async_copy(ck_ref.at[layer, pid, :, hh, :], kbuf.at[slot, i],
                                                        sem.at[0, slot]))
                    copies.append(pltpu.make_async_copy(cv_ref.at[layer, pid, :, hh, :], vbuf.at[slot, i],
                                                        sem.at[1, slot]))
        return copies

    @pl.when(step == 0)
    def _():
        for c in page_copies(step, 0):
            c.start()

    @pl.when(step + 1 < n_steps)
    def _():
        for c in page_copies(step + 1, (step + 1) & 1):
            c.start()

    slot = step & 1
    for c in page_copies(step, slot):
        c.wait()

    far_bias = rb_ref[REL_BUCKETS - 1, h]
    kn = _pad_frame(kn_ref[...]).astype(BF16)
    vn = _pad_frame(vn_ref[...]).astype(BF16)
    lane = lax.broadcasted_iota(jnp.int32, (1, FRAME), 1)
    for s in range(DEC_SEQ):
        row = b * DEC_SEQ + s
        q8 = jnp.broadcast_to(q_ref[s:s + 1, :], (8, C_DIM)).astype(BF16)
        logits = []
        for r in range(MOBA_TOPK):
            newest = block_of(h, b, s, r) == N_PAST_BLK - 1
            for j in range(PAGES_PER_BLK):
                kp = kbuf[slot, (s * MOBA_TOPK + r) * PAGES_PER_BLK + j].astype(BF16)
                lg = lax.dot_general(q8, kp, _NT, preferred_element_type=F32)[0:1] * scale
                bias = jnp.where(newest, bs_ref[s:s + 1, j * PAGE_SIZE:(j + 1) * PAGE_SIZE], far_bias)
                logits.append(lg + bias)
        lg = lax.dot_general(q8, kn, _NT, preferred_element_type=F32)[0:1] * scale
        ob = jnp.zeros((1, FRAME), F32)
        for d in range(DEC_SEQ):
            ob = jnp.where(lane == row - d, rb_ref[d, h], ob)
        valid = (lane >= b * DEC_SEQ) & (lane <= row)
        logits.append(jnp.where(valid, lg + ob, NEG_BIG))

        mx = jnp.max(functools.reduce(jnp.maximum, logits), axis=1, keepdims=True)
        ps = [jnp.exp(l - mx) for l in logits]
        denom = functools.reduce(lambda x, y: x + y, [jnp.sum(p, axis=1, keepdims=True) for p in ps])
        vals = [vbuf[slot, s * MOBA_TOPK * PAGES_PER_BLK + i].astype(BF16)
                for i in range(MOBA_TOPK * PAGES_PER_BLK)] + [vn]
        out = jnp.zeros((8, C_DIM), F32)
        for p, vv in zip(ps, vals):
            p8 = jnp.broadcast_to(p, (8, p.shape[1])).astype(BF16)
            out = out + jnp.dot(p8, vv, preferred_element_type=F32)
        o_ref[s:s + 1, :] = out[0:1] / denom


def _moba_sample(proj_s, cache_k, cache_v, idx_flat, pt_flat, rel_bias, bias_s, layer):
    n_pg = DEC_SEQ * MOBA_TOPK * PAGES_PER_BLK
    proj3 = proj_s.reshape(DEC_BATCH, DEC_SEQ, N_MAIN)
    out = pl.pallas_call(
        functools.partial(_moba_sample_body, layer=layer),
        out_shape=jax.ShapeDtypeStruct((DEC_BATCH, DEC_SEQ, C_WIDTH), F32),
        grid_spec=pltpu.PrefetchScalarGridSpec(
            num_scalar_prefetch=2,
            grid=(C_HEADS, DEC_BATCH),
            in_specs=[pl.BlockSpec(memory_space=pltpu.SMEM),
                      pl.BlockSpec((None, DEC_SEQ, C_DIM), lambda h, b, idx, pt: (b, 0, OFF_CQ + h)),
                      pl.BlockSpec((N_TOK_S, C_DIM), lambda h, b, idx, pt: (0, OFF_CK + h)),
                      pl.BlockSpec((N_TOK_S, C_DIM), lambda h, b, idx, pt: (0, OFF_CV + h)),
                      pl.BlockSpec((None, 8, MOBA_BLOCK), lambda h, b, idx, pt: (h, 0, 0)),
                      pl.BlockSpec(memory_space=pl.ANY),
                      pl.BlockSpec(memory_space=pl.ANY)],
            out_specs=pl.BlockSpec((None, DEC_SEQ, C_DIM), lambda h, b, idx, pt: (b, 0, h)),
            scratch_shapes=[pltpu.VMEM((2, n_pg, PAGE_SIZE, C_DIM), F32),
                            pltpu.VMEM((2, n_pg, PAGE_SIZE, C_DIM), F32),
                            pltpu.SemaphoreType.DMA((2, 2))]),
        compiler_params=_cparams(("arbitrary", "arbitrary")),
        name="moba_sample",
    )(idx_flat, pt_flat, rel_bias, proj3, proj_s, proj_s, bias_s, cache_k, cache_v)
    return out.reshape(N_TOK_S, C_WIDTH)


def _merge_body(ya_ref, yb_ref, yc_ref, wa_ref, wb_ref, wc_ref, g0_ref, g1_ref, g2_ref, o_ref):
    acc = g0_ref[...] * jnp.dot(ya_ref[...], wa_ref[...], preferred_element_type=F32)
    acc = acc + g1_ref[...] * jnp.dot(yb_ref[...], wb_ref[...], preferred_element_type=F32)
    acc = acc + g2_ref[...] * jnp.dot(yc_ref[...], wc_ref[...], preferred_element_type=F32)
    o_ref[...] = acc.astype(o_ref.dtype)


def _merge(ya, yb, yc, gates, w_a, w_b, w_c, layer, *, tm, tn):
    m = ya.shape[0]
    nb = D_MODEL // tn
    act = lambda width: pl.BlockSpec((tm, width), lambda j, i: (i, 0))
    wgt = lambda width: pl.BlockSpec((None, width, tn), lambda j, i: (layer, 0, j))
    gate = lambda br: pl.BlockSpec((tm, tn), lambda j, i: (i, br * nb + j))
    return pl.pallas_call(
        _merge_body,
        out_shape=jax.ShapeDtypeStruct((m, D_MODEL), BF16),
        grid=(nb, m // tm),
        in_specs=[act(A_WIDTH), act(B_WIDTH), act(C_WIDTH), wgt(A_WIDTH), wgt(B_WIDTH), wgt(C_WIDTH),
                  gate(0), gate(1), gate(2)],
        out_specs=pl.BlockSpec((tm, tn), lambda j, i: (i, j)),
        compiler_params=_cparams(("parallel", "parallel")),
        name="branch_merge",
    )(ya, yb, yc, w_a, w_b, w_c, gates, gates, gates)


def kernel(x_prompt, x_sample, cache_k, cache_v, state_hgrn, state_conv, page_table, w_in, b_gate,
           conv_w, hgrn_lb_logits, hgrn_norm_g, w_a_up, w_b_up, w_c_up, w_o, rel_bias, g_pre_mix,
           g_post_mix, g_pre_mlp, g_post_mlp, w_mlp_up, w_mlp_down):
    consts = {k: jnp.asarray(v) for k, v in _constants().items()}
    for name in ("m3_p", "emask_p", "m3_s", "emask_s"):
        consts[name] = consts[name].astype(BF16)

    w_in_b, w_a_b, w_b_b, w_c_b = (w.astype(BF16) for w in (w_in, w_a_up, w_b_up, w_c_up))
    w_o_b, w_up_b, w_dn_b = (w.astype(BF16) for w in (w_o, w_mlp_up, w_mlp_down))
    vec = lambda g: g.reshape(DEPTH, 1, g.shape[-1])
    b_gate3, ng3 = vec(b_gate), vec(hgrn_norm_g)
    gpm, gqm, gpl, gql = vec(g_pre_mix), vec(g_post_mix), vec(g_pre_mlp), vec(g_post_mlp)
    pt_flat = page_table.reshape(-1)

    lbp = _lower_bound_params(hgrn_lb_logits)
    bown, bprev, bias_s = _bias_tables(rel_bias, consts)
    means = _cache_block_means(cache_k, pt_flat)

    groups = {
        "p": dict(x=x_prompt.reshape(N_TOK_P, D_MODEL), tm_row=128, mm=dict(tm=1024, tn=1024, tk=D_MODEL),
                  mm_down=dict(tm=2048, tn=1024, tk=1024), merge=dict(tm=512, tn=1024)),
        "s": dict(x=x_sample.reshape(N_TOK_S, D_MODEL), tm_row=N_TOK_S, mm=dict(tm=N_TOK_S, tn=1024, tk=D_MODEL),
                  mm_down=dict(tm=N_TOK_S, tn=1024, tk=D_MODEL), merge=dict(tm=N_TOK_S, tn=1024)),
    }
    for g in groups.values():
        g["h"] = _prenorm(g["x"], gpm, 0, tm=g["tm_row"])
    outs = {k: [] for k in ("kp", "vp", "ks", "vs", "sp", "ss", "cp", "cs")}

    for l in range(DEPTH):
        for tag, g in groups.items():
            mm = g["mm"]
            proj = _matmul(g["h"], w_in_b, l, col0=0, n_cols=N_MAIN, name="in_proj", **mm)
            gates = _matmul(g["h"], w_in_b, l, col0=N_MAIN, n_cols=N_GATE, epilogue="sigmoid_bias",
                            bias=b_gate3, name="in_proj_gates", **mm)
            if tag == "p":
                ya, st = _hgrn_prompt(proj, lbp, ng3, l, consts)
                yb, cn = _shortconv(proj.reshape(BATCH, SEQ, N_MAIN), conv_w, l, None, tc=256, out_dtype=BF16)
                yb = yb.reshape(N_TOK_P, B_WIDTH)
                yc = _moba_prompt(proj, rel_bias, bown, bprev)
                kv_shape = (BATCH, SEQ, C_HEADS, C_DIM)
            else:
                ya, st = _hgrn_sample(proj, state_hgrn, lbp, ng3, l, consts)
                yb, cn = _shortconv(proj.reshape(DEC_BATCH, DEC_SEQ, N_MAIN), conv_w, l, state_conv,
                                    tc=B_WIDTH, out_dtype=F32)
                yb = yb.reshape(N_TOK_S, B_WIDTH).astype(BF16)
                idx = _moba_gate(proj, means, l)
                idx_flat = idx[:, :, :MOBA_TOPK].reshape(-1)
                yc = _moba_sample(proj, cache_k, cache_v, idx_flat, pt_flat, rel_bias, bias_s, l).astype(BF16)
                kv_shape = (DEC_BATCH, DEC_SEQ, C_HEADS, C_DIM)
            outs["k" + tag].append(proj[:, OFF_CK * 128:OFF_CK * 128 + C_WIDTH].reshape(kv_shape))
            outs["v" + tag].append(proj[:, OFF_CV * 128:OFF_CV * 128 + C_WIDTH].reshape(kv_shape))
            outs["s" + tag].append(st)
            outs["c" + tag].append(cn)

            merged = _merge(ya, yb, yc, gates, w_a_b, w_b_b, w_c_b, l, **g["merge"])
            mix = _matmul(merged, w_o_b, l, name="out_proj", **mm)
            x, hm = _post(g["x"], mix, gqm, l, gpl, l, tm=g["tm_row"])
            up = _matmul(hm, w_up_b, l, epilogue="relu2", out_dtype=BF16, name="mlp_up", **mm)
            dn = _matmul(up, w_dn_b, l, name="mlp_down", **g["mm_down"])
            if l + 1 < DEPTH:
                g["x"], g["h"] = _post(x, dn, gql, l, gpm, l + 1, tm=g["tm_row"])
            else:
                g["x"], g["h"] = _post(x, dn, gql, l, tm=g["tm_row"])

    return (groups["p"]["x"].reshape(BATCH, SEQ, D_MODEL),
            groups["s"]["x"].reshape(DEC_BATCH, DEC_SEQ, D_MODEL),
            jnp.stack(outs["kp"]), jnp.stack(outs["vp"]), jnp.stack(outs["ks"]), jnp.stack(outs["vs"]),
            jnp.stack(outs["sp"]), jnp.stack(outs["ss"]), jnp.stack(outs["cp"]), jnp.stack(outs["cs"]))
```

```python
import functools
import math

import numpy as np
import jax
import jax.numpy as jnp
from jax import lax
from jax.experimental import pallas as pl
from jax.experimental.pallas import tpu as pltpu

F32 = jnp.float32
BF16 = jnp.bfloat16

D_MODEL = 4096
BATCH = 4
SEQ = 2048
DEPTH = 4
DEC_BATCH = 8
DEC_SEQ = 4
PAST_LEN = 8192
PAGE_SIZE = 128
A_HEADS = 8
A_DIM = 128
A_WIDTH = A_HEADS * A_DIM
A_CHUNK = 16
B_WIDTH = D_MODEL // 4
C_HEADS = 16
C_DIM = 128
C_WIDTH = C_HEADS * C_DIM
MOBA_BLOCK = 256
MOBA_TOPK = 3
REL_BUCKETS = 32
REL_MAX_DIST = 128
D_FF = 4 * D_MODEL
NORM_EPS = 1e-6
NEG_BIG = -1e30
LB_FLOOR = 1e-30

N_TOK_P = BATCH * SEQ
N_TOK_S = DEC_BATCH * DEC_SEQ
N_MAIN = 4 * A_WIDTH + 3 * B_WIDTH + 3 * C_WIDTH
N_GATE = 3 * D_MODEL
OFF_AQ, OFF_AF, OFF_AI, OFF_AG = 0, 8, 16, 24
OFF_BB, OFF_BC, OFF_BX = 32, 40, 48
OFF_CQ, OFF_CK, OFF_CV = 56, 72, 88
N_PAGES = PAST_LEN // PAGE_SIZE
N_PAST_BLK = PAST_LEN // MOBA_BLOCK
PAGES_PER_BLK = MOBA_BLOCK // PAGE_SIZE
FRAME = 128
HGRN_UNROLL = 4

V7X_VMEM_LIMIT = 48 * 1024 * 1024
V7X_VMEM_BIG = 56 * 1024 * 1024

_NT = (((1,), (1,)), ((), ()))


def _cparams(sem, vmem=V7X_VMEM_LIMIT):
    return pltpu.CompilerParams(dimension_semantics=sem, vmem_limit_bytes=vmem)


def _sigmoid(x):
    return 1.0 / (1.0 + jnp.exp(-x))


def _split3(x):
    hi = x.astype(BF16)
    r1 = x - hi.astype(F32)
    lo = r1.astype(BF16)
    lo2 = (r1 - lo.astype(F32)).astype(BF16)
    return hi, lo, lo2


def _split2(x):
    hi = x.astype(BF16)
    lo = (x - hi.astype(F32)).astype(BF16)
    return hi, lo


def _rel_bucket_np(rel):
    n = np.maximum(rel, 0)
    max_exact = REL_BUCKETS // 2
    nf = np.maximum(n, max_exact).astype(np.float32)
    large = max_exact + (np.log(nf / max_exact) / math.log(REL_MAX_DIST / max_exact)
                         * (REL_BUCKETS - max_exact)).astype(np.int32)
    large = np.clip(large, max_exact, REL_BUCKETS - 1)
    return np.where(n < max_exact, n, large).astype(np.int32)


def _chunk_mats(n, chunk):
    r = np.arange(n)[:, None]
    c = np.arange(n)[None, :]
    same = (r // chunk) == (c // chunk)
    t_cum = same & (c <= r)
    t_last = same
    t_ref = same & ((c % chunk) <= chunk // 2)
    m3 = np.concatenate([t_cum, t_last, t_ref], axis=0).astype(np.float32)
    return m3, t_cum.astype(np.float32)


@functools.lru_cache(maxsize=None)
def _constants():
    m3_p, causal_p = _chunk_mats(MOBA_BLOCK, A_CHUNK)
    rows = np.arange(16 * A_DIM)[:, None]
    cols = np.arange(256)[None, :]
    emask_p = ((rows // A_DIM) == (cols // A_CHUNK)).astype(np.float32)
    m3_s, causal_s = _chunk_mats(FRAME, DEC_SEQ)
    rows = np.arange(DEC_BATCH * A_DIM)[:, None]
    cols = np.arange(FRAME)[None, :]
    emask_s = (((rows // A_DIM) == (cols // DEC_SEQ)) & (cols < N_TOK_S)).astype(np.float32)
    kl = np.arange(MOBA_BLOCK)[:, None]
    ql = np.arange(MOBA_BLOCK)[None, :]
    bucket_own = _rel_bucket_np(ql - kl)
    bucket_prev = _rel_bucket_np(MOBA_BLOCK + ql - kl)
    s = np.arange(8)[:, None]
    t = np.arange(MOBA_BLOCK)[None, :]
    bucket_s = _rel_bucket_np(MOBA_BLOCK + s - t)
    return dict(m3_p=m3_p, causal_p=causal_p, emask_p=emask_p, m3_s=m3_s, causal_s=causal_s,
                emask_s=emask_s, bucket_own=bucket_own, bucket_prev=bucket_prev, bucket_s=bucket_s)


def _mm_body(*refs, nk, epilogue):
    if epilogue == "sigmoid_bias":
        a_ref, w_ref, b_ref, o_ref = refs
    else:
        a_ref, w_ref, o_ref = refs
        b_ref = None

    def finish(r):
        if epilogue == "sigmoid_bias":
            r = _sigmoid(r + b_ref[...])
        elif epilogue == "relu2":
            r = jnp.square(jnp.maximum(r, 0.0))
        o_ref[...] = r.astype(o_ref.dtype)

    if nk == 1:
        finish(jnp.dot(a_ref[...], w_ref[...], preferred_element_type=F32))
        return

    @pl.when(pl.program_id(2) == 0)
    def _():
        o_ref[...] = jnp.zeros(o_ref.shape, F32)

    o_ref[...] += jnp.dot(a_ref[...], w_ref[...], preferred_element_type=F32)


def _matmul(a, w, layer, *, col0=0, n_cols=None, tm, tn, tk, epilogue="none", bias=None,
            out_dtype=F32, name):
    m, kdim = a.shape
    n_cols = w.shape[2] if n_cols is None else n_cols
    assert m % tm == 0 and kdim % tk == 0 and n_cols % tn == 0 and col0 % tn == 0
    nk = kdim // tk
    assert nk == 1 or (epilogue == "none" and out_dtype == F32)
    cb = col0 // tn
    in_specs = [pl.BlockSpec((tm, tk), lambda i, j, k: (i, k)),
                pl.BlockSpec((None, tk, tn), lambda i, j, k: (layer, k, j + cb))]
    args = [a, w]
    if epilogue == "sigmoid_bias":
        in_specs.append(pl.BlockSpec((None, 1, tn), lambda i, j, k: (layer, 0, j)))
        args.append(bias)
    return pl.pallas_call(
        functools.partial(_mm_body, nk=nk, epilogue=epilogue),
        out_shape=jax.ShapeDtypeStruct((m, n_cols), out_dtype),
        grid=(m // tm, n_cols // tn, nk),
        in_specs=in_specs,
        out_specs=pl.BlockSpec((tm, tn), lambda i, j, k: (i, j)),
        compiler_params=_cparams(("parallel", "parallel", "arbitrary"), V7X_VMEM_BIG),
        name=name,
    )(*args)


def _rms(x, g):
    return x * lax.rsqrt(jnp.mean(x * x, axis=-1, keepdims=True) + NORM_EPS) * g


def _prenorm_body(x_ref, g_ref, h_ref):
    h_ref[...] = _rms(x_ref[...], g_ref[...]).astype(h_ref.dtype)


def _prenorm(x, g, layer, *, tm):
    m = x.shape[0]
    return pl.pallas_call(
        _prenorm_body,
        out_shape=jax.ShapeDtypeStruct((m, D_MODEL), BF16),
        grid=(m // tm,),
        in_specs=[pl.BlockSpec((tm, D_MODEL), lambda i: (i, 0)),
                  pl.BlockSpec((None, 1, D_MODEL), lambda i: (layer, 0, 0))],
        out_specs=pl.BlockSpec((tm, D_MODEL), lambda i: (i, 0)),
        compiler_params=_cparams(("parallel",)),
        name="prenorm",
    )(x, g)


def _post_body(x_ref, y_ref, gpost_ref, *rest, with_next):
    xn = x_ref[...] + _rms(y_ref[...], gpost_ref[...])
    if with_next:
        gnext_ref, xo_ref, ho_ref = rest
        ho_ref[...] = _rms(xn, gnext_ref[...]).astype(ho_ref.dtype)
    else:
        (xo_ref,) = rest
    xo_ref[...] = xn


def _post(x, y, gpost, layer, gnext=None, next_layer=None, *, tm):
    m = x.shape[0]
    with_next = gnext is not None
    row = pl.BlockSpec((tm, D_MODEL), lambda i: (i, 0))
    in_specs = [row, row, pl.BlockSpec((None, 1, D_MODEL), lambda i: (layer, 0, 0))]
    args = [x, y, gpost]
    out_shape = [jax.ShapeDtypeStruct((m, D_MODEL), F32)]
    out_specs = [row]
    if with_next:
        in_specs.append(pl.BlockSpec((None, 1, D_MODEL), lambda i: (next_layer, 0, 0)))
        args.append(gnext)
        out_shape.append(jax.ShapeDtypeStruct((m, D_MODEL), BF16))
        out_specs.append(row)
    res = pl.pallas_call(
        functools.partial(_post_body, with_next=with_next),
        out_shape=out_shape,
        grid=(m // tm,),
        in_specs=in_specs,
        out_specs=out_specs,
        compiler_params=_cparams(("parallel",)),
        name="post_norm",
    )(*args)
    return (res[0], res[1]) if with_next else (res[0], None)


def _lb_body(x_ref, o_ref):
    x = x_ref[...]
    rows = [x[l:l + 1] for l in range(DEPTH)]
    mx = functools.reduce(jnp.maximum, rows)
    es = [jnp.exp(r - mx) for r in rows]
    tot = functools.reduce(lambda a, b: a + b, es)
    ps = [e / tot for e in es]
    cum = None
    for l in range(DEPTH):
        cum = ps[l] if cum is None else cum + ps[l]
        lb = jnp.clip(cum - ps[0], 0.0, 1.0)
        o_ref[3 * l:3 * l + 1, :] = jnp.log(jnp.maximum(lb, LB_FLOOR))
        o_ref[3 * l + 1:3 * l + 2, :] = jnp.log1p(-lb)
        o_ref[3 * l + 2:3 * l + 3, :] = 1.0 - lb


def _lower_bound_params(lb_logits):
    out = pl.pallas_call(
        _lb_body,
        out_shape=jax.ShapeDtypeStruct((3 * DEPTH, A_WIDTH), F32),
        name="hgrn_lower_bounds",
    )(lb_logits)
    return out.reshape(DEPTH, 3, A_HEADS, 1, A_DIM)


def _hgrn_gates(aq, z, la, l1m, oml):
    q = aq * _sigmoid(aq)
    log_sig = jnp.minimum(z, 0.0) - jnp.log1p(jnp.exp(-jnp.abs(z)))
    b = l1m + log_sig
    log_f = jnp.maximum(la, b) + jnp.log1p(jnp.exp(-jnp.abs(la - b)))
    k = oml / (1.0 + jnp.exp(z))
    return q, k, log_f


def _chunk_sums(m3, log_f, n):
    hi, lo, lo2 = _split3(log_f)
    c3 = (jnp.dot(m3, hi, preferred_element_type=F32)
          + jnp.dot(m3, lo, preferred_element_type=F32)
          + jnp.dot(m3, lo2, preferred_element_type=F32))
    return c3[0:n], c3[n:2 * n], c3[2 * n:3 * n]


def _hgrn_prompt_body(aq_ref, af_ref, ai_ref, ag_ref, lbp_ref, ng_ref, m3_ref, cmask_ref, emask_ref,
                      ya_ref, st_ref, o_scr, qc_scr, dec_scr, ut_scr):
    blk = MOBA_BLOCK
    n_blk = SEQ // blk
    cpb = blk // A_CHUNK
    la = lbp_ref[0]
    l1m = lbp_ref[1]
    oml = lbp_ref[2]

    def block_body(j, carry):
        rows = pl.ds(pl.multiple_of(j * blk, blk), blk)
        q, k, log_f = _hgrn_gates(aq_ref[rows, :], af_ref[rows, :], la, l1m, oml)
        v = ai_ref[rows, :]
        cum, last_b, ref_b = _chunk_sums(m3_ref[...], log_f, blk)
        q_rel = (q * jnp.exp(cum - ref_b)).astype(BF16)
        k_rel = (k * jnp.exp(ref_b - cum)).astype(BF16)
        k_last = (k * jnp.exp(last_b - cum)).astype(BF16)
        a = lax.dot_general(q_rel, k_rel, _NT, preferred_element_type=F32)
        a = jnp.where(cmask_ref[...] > 0.0, a, 0.0)
        o_scr[rows, :] = jnp.dot(a.astype(BF16), v.astype(BF16), preferred_element_type=F32)
        qc_scr[rows, :] = (q * jnp.exp(cum)).astype(BF16)
        dec_scr[rows, :] = jnp.exp(last_b)
        vt = v.T.astype(BF16)
        vt_exp = jnp.concatenate([vt] * cpb, axis=0) * emask_ref[...]
        ut_rows = pl.ds(pl.multiple_of(j * (cpb * A_DIM), cpb * A_DIM), cpb * A_DIM)
        ut_scr[ut_rows, :] = jnp.dot(vt_exp, k_last, preferred_element_type=F32)
        return carry

    lax.fori_loop(0, n_blk, block_body, 0, unroll=HGRN_UNROLL)

    def scan_body(n, st):
        rows = pl.ds(pl.multiple_of(n * A_DIM, A_DIM), A_DIM)
        u = ut_scr[rows, :]
        ut_scr[rows, :] = st
        d = dec_scr[pl.ds(pl.multiple_of(n * A_CHUNK, A_CHUNK), 8), :][0:1, :]
        return st * d + u

    st = lax.fori_loop(0, SEQ // A_CHUNK, scan_body, jnp.zeros((A_DIM, A_DIM), F32), unroll=4)
    st_ref[...] = st.T

    ng = ng_ref[...]

    def out_body(j, carry):
        rows = pl.ds(pl.multiple_of(j * blk, blk), blk)
        ut_rows = pl.ds(pl.multiple_of(j * (cpb * A_DIM), cpb * A_DIM), cpb * A_DIM)
        starts = ut_scr[ut_rows, :].astype(BF16)
        r = lax.dot_general(qc_scr[rows, :], starts, _NT, preferred_element_type=F32)
        o_inter = jnp.concatenate(
            [r[n * A_CHUNK:(n + 1) * A_CHUNK, n * A_DIM:(n + 1) * A_DIM] for n in range(cpb)], axis=0)
        o = _rms(o_scr[rows, :] + o_inter, ng)
        ag = ag_ref[rows, :]
        ya_ref[rows, :] = (o * (ag * _sigmoid(ag))).astype(ya_ref.dtype)
        return carry

    lax.fori_loop(0, n_blk, out_body, 0, unroll=HGRN_UNROLL)


def _hgrn_prompt(proj, lbp, norm_g, layer, consts):
    col = lambda off: pl.BlockSpec((SEQ, A_DIM), lambda b, h: (b, off + h))
    whole = lambda shape: pl.BlockSpec(shape, lambda b, h: (0,) * len(shape))
    return pl.pallas_call(
        _hgrn_prompt_body,
        out_shape=[jax.ShapeDtypeStruct((N_TOK_P, A_WIDTH), BF16),
                   jax.ShapeDtypeStruct((BATCH, A_HEADS, A_DIM, A_DIM), F32)],
        grid=(BATCH, A_HEADS),
        in_specs=[col(OFF_AQ), col(OFF_AF), col(OFF_AI), col(OFF_AG),
                  pl.BlockSpec((None, 3, None, 1, A_DIM), lambda b, h: (layer, 0, h, 0, 0)),
                  pl.BlockSpec((None, 1, A_DIM), lambda b, h: (layer, 0, 0)),
                  whole((3 * MOBA_BLOCK, MOBA_BLOCK)), whole((MOBA_BLOCK, MOBA_BLOCK)),
                  whole((16 * A_DIM, MOBA_BLOCK))],
        out_specs=[pl.BlockSpec((SEQ, A_DIM), lambda b, h: (b, h)),
                   pl.BlockSpec((None, None, A_DIM, A_DIM), lambda b, h: (b, h, 0, 0))],
        scratch_shapes=[pltpu.VMEM((SEQ, A_DIM), F32), pltpu.VMEM((SEQ, A_DIM), BF16),
                        pltpu.VMEM((SEQ, A_DIM), F32), pltpu.VMEM((SEQ // A_CHUNK * A_DIM, A_DIM), F32)],
        compiler_params=_cparams(("parallel", "parallel")),
        name="hgrn_prompt",
    )(proj, proj, proj, proj, lbp, norm_g, consts["m3_p"], consts["causal_p"], consts["emask_p"])


def _pad_frame(x):
    return jnp.concatenate([x, jnp.zeros((FRAME - x.shape[0], x.shape[1]), x.dtype)], axis=0)


def _hgrn_sample_body(aq_ref, af_ref, ai_ref, ag_ref, s0_ref, lbp_ref, ng_ref, m3_ref, cmask_ref,
                      emask_ref, ya_ref, sn_ref):
    la = lbp_ref[0]
    l1m = lbp_ref[1]
    oml = lbp_ref[2]
    q, k, log_f = _hgrn_gates(_pad_frame(aq_ref[...]), _pad_frame(af_ref[...]), la, l1m, oml)
    v = _pad_frame(ai_ref[...])
    cum, last_b, ref_b = _chunk_sums(m3_ref[...], log_f, FRAME)
    q_rel = (q * jnp.exp(cum - ref_b)).astype(BF16)
    k_rel = (k * jnp.exp(ref_b - cum)).astype(BF16)
    k_last = (k * jnp.exp(last_b - cum)).astype(BF16)
    a = lax.dot_general(q_rel, k_rel, _NT, preferred_element_type=F32)
    a = jnp.where(cmask_ref[...] > 0.0, a, 0.0)
    o_intra = jnp.dot(a.astype(BF16), v.astype(BF16), preferred_element_type=F32)
    qc = (q * jnp.exp(cum)).astype(BF16)
    dec = jnp.exp(last_b)
    vt = v.T.astype(BF16)
    vt_exp = jnp.concatenate([vt] * DEC_BATCH, axis=0) * emask_ref[...]
    ut = jnp.dot(vt_exp, k_last, preferred_element_type=F32)
    s0t = jnp.concatenate([s0_ref[b].T for b in range(DEC_BATCH)], axis=0)
    r = lax.dot_general(qc, s0t.astype(BF16), _NT, preferred_element_type=F32)
    row_b = lax.broadcasted_iota(jnp.int32, (FRAME, A_DIM), 0) >> int(math.log2(DEC_SEQ))
    o_inter = jnp.zeros((FRAME, A_DIM), F32)
    for b in range(DEC_BATCH):
        o_inter = jnp.where(row_b == b, r[:, b * A_DIM:(b + 1) * A_DIM], o_inter)
        d = dec[b * DEC_SEQ:b * DEC_SEQ + 1, :]
        snt = s0t[b * A_DIM:(b + 1) * A_DIM, :] * d + ut[b * A_DIM:(b + 1) * A_DIM, :]
        sn_ref[b] = snt.T
    o = _rms(o_intra + o_inter, ng_ref[...])
    ag = _pad_frame(ag_ref[...])
    ya_ref[...] = (o * (ag * _sigmoid(ag)))[0:N_TOK_S].astype(ya_ref.dtype)


def _hgrn_sample(proj, state, lbp, norm_g, layer, consts):
    col = lambda off: pl.BlockSpec((N_TOK_S, A_DIM), lambda h: (0, off + h))
    whole = lambda shape: pl.BlockSpec(shape, lambda h: (0,) * len(shape))
    return pl.pallas_call(
        _hgrn_sample_body,
        out_shape=[jax.ShapeDtypeStruct((N_TOK_S, A_WIDTH), BF16),
                   jax.ShapeDtypeStruct((DEC_BATCH, A_HEADS, A_DIM, A_DIM), F32)],
        grid=(A_HEADS,),
        in_specs=[col(OFF_AQ), col(OFF_AF), col(OFF_AI), col(OFF_AG),
                  pl.BlockSpec((None, DEC_BATCH, None, A_DIM, A_DIM), lambda h: (layer, 0, h, 0, 0)),
                  pl.BlockSpec((None, 3, None, 1, A_DIM), lambda h: (layer, 0, h, 0, 0)),
                  pl.BlockSpec((None, 1, A_DIM), lambda h: (layer, 0, 0)),
                  whole((3 * FRAME, FRAME)), whole((FRAME, FRAME)), whole((DEC_BATCH * A_DIM, FRAME))],
        out_specs=[pl.BlockSpec((N_TOK_S, A_DIM), lambda h: (0, h)),
                   pl.BlockSpec((DEC_BATCH, None, A_DIM, A_DIM), lambda h: (0, h, 0, 0))],
        compiler_params=_cparams(("parallel",)),
        name="hgrn_sample",
    )(proj, proj, proj, proj, state, lbp, norm_g, consts["m3_s"], consts["causal_s"], consts["emask_s"])


def _conv_body(bb_ref, bc_ref, bx_ref, w_ref, *rest, t, has_prev):
    if has_prev:
        prev_ref, y_ref, cn_ref, scr = rest
    else:
        y_ref, cn_ref, scr = rest
    u = bc_ref[...] * bx_ref[...]
    scr[0:8, :] = jnp.zeros((8, scr.shape[1]), F32)
    if has_prev:
        scr[6:8, :] = prev_ref[...]
    scr[8:8 + t, :] = u
    w = w_ref[...]
    y = w[0:1] * scr[6:6 + t, :] + w[1:2] * scr[7:7 + t, :] + w[2:3] * u
    y_ref[...] = (bb_ref[...] * y).astype(y_ref.dtype)
    cn_ref[...] = scr[6 + t:8 + t, :]


def _shortconv(proj3, conv_w, layer, prev, *, tc, out_dtype):
    bsz, t, _ = proj3.shape
    has_prev = prev is not None
    lanes = lambda off: pl.BlockSpec((None, t, tc), lambda b, c: (b, 0, off * 128 // tc + c))
    in_specs = [lanes(OFF_BB), lanes(OFF_BC), lanes(OFF_BX),
                pl.BlockSpec((None, 3, tc), lambda b, c: (layer, 0, c))]
    args = [proj3, proj3, proj3, conv_w]
    if has_prev:
        in_specs.append(pl.BlockSpec((None, None, 2, tc), lambda b, c: (layer, b, 0, c)))
        args.append(prev)
    return pl.pallas_call(
        functools.partial(_conv_body, t=t, has_prev=has_prev),
        out_shape=[jax.ShapeDtypeStruct((bsz, t, B_WIDTH), out_dtype),
                   jax.ShapeDtypeStruct((bsz, 2, B_WIDTH), F32)],
        grid=(bsz, B_WIDTH // tc),
        in_specs=in_specs,
        out_specs=[pl.BlockSpec((None, t, tc), lambda b, c: (b, 0, c)),
                   pl.BlockSpec((None, 2, tc), lambda b, c: (b, 0, c))],
        scratch_shapes=[pltpu.VMEM((t + 8, tc), F32)],
        compiler_params=_cparams(("parallel", "parallel")),
        name="shortconv",
    )(*args)


def _bias_body(rb_ref, bo_ref, bp_ref, bs_ref, own_ref, prev_ref, smp_ref):
    h = pl.program_id(0)

    def lookup(bucket):
        acc = jnp.zeros(bucket.shape, F32)
        for b in range(REL_BUCKETS):
            acc = jnp.where(bucket == b, rb_ref[b, h], acc)
        return acc

    blk = MOBA_BLOCK
    key_i = lax.broadcasted_iota(jnp.int32, (blk, blk), 0)
    qry_i = lax.broadcasted_iota(jnp.int32, (blk, blk), 1)
    own_ref[...] = jnp.where(key_i <= qry_i, lookup(bo_ref[...]), NEG_BIG)
    prev_ref[...] = lookup(bp_ref[...])
    smp_ref[...] = lookup(bs_ref[...])


def _bias_tables(rel_bias, consts):
    whole = lambda shape: pl.BlockSpec(shape, lambda h: (0,) * len(shape))
    blk = MOBA_BLOCK
    return pl.pallas_call(
        _bias_body,
        out_shape=[jax.ShapeDtypeStruct((C_HEADS, blk, blk), F32),
                   jax.ShapeDtypeStruct((C_HEADS, blk, blk), F32),
                   jax.ShapeDtypeStruct((C_HEADS, 8, blk), F32)],
        grid=(C_HEADS,),
        in_specs=[pl.BlockSpec(memory_space=pltpu.SMEM),
                  whole((blk, blk)), whole((blk, blk)), whole((8, blk))],
        out_specs=[pl.BlockSpec((None, blk, blk), lambda h: (h, 0, 0)),
                   pl.BlockSpec((None, blk, blk), lambda h: (h, 0, 0)),
                   pl.BlockSpec((None, 8, blk), lambda h: (h, 0, 0))],
        compiler_params=_cparams(("parallel",)),
        name="rel_bias_tables",
    )(rel_bias, consts["bucket_own"], consts["bucket_prev"], consts["bucket_s"])


def _moba_prompt_body(rb_ref, q_ref, k_ref, v_ref, bown_ref, bprev_ref, *rest, layer):
    o_ref, ko_ref, vo_ref, kbf, vtbf, sem = rest[-6:]
    blk = MOBA_BLOCK
    n_blk = SEQ // blk
    b = pl.program_id(0)
    h = pl.program_id(1)
    scale = C_DIM ** -0.5
    far_bias = rb_ref[REL_BUCKETS - 1, h]

    out_copies = [pltpu.make_async_copy(k_ref, ko_ref.at[layer, b, :, h, :], sem.at[0]),
                  pltpu.make_async_copy(v_ref, vo_ref.at[layer, b, :, h, :], sem.at[1])]
    for c in out_copies:
        c.start()

    means = []
    for i in range(n_blk):
        kb = k_ref[i * blk:(i + 1) * blk, :]
        kbf[i * blk:(i + 1) * blk, :] = kb.astype(BF16)
        means.append(jnp.sum(kb, axis=0, keepdims=True) / blk)
        vtbf[:, i * blk:(i + 1) * blk] = v_ref[i * blk:(i + 1) * blk, :].T.astype(BF16)
    means = jnp.concatenate(means + [jnp.zeros((n_blk, C_DIM), F32)], axis=0)
    mh, ml = _split2(means)
    blk_i = lax.broadcasted_iota(jnp.int32, (n_blk, blk), 0)

    for qi in range(n_blk):
        q = q_ref[qi * blk:(qi + 1) * blk, :]
        n_keys = (qi + 1) * blk
        s = lax.dot_general(kbf[0:n_keys, :], q.astype(BF16), _NT, preferred_element_type=F32) * scale
        if qi >= 1:
            qh, ql = _split2(q)
            gate = (lax.dot_general(mh, qh, _NT, preferred_element_type=F32)
                    + lax.dot_general(mh, ql, _NT, preferred_element_type=F32)
                    + lax.dot_general(ml, qh, _NT, preferred_element_type=F32))[0:n_blk]
            rank = jnp.zeros((n_blk, blk), jnp.int32)
            for jp in range(qi):
                gj = gate[jp:jp + 1, :]
                beats = (gj > gate) | ((gj == gate) & (jp < blk_i))
                rank = rank + jnp.where(beats, 1, 0)
            keep = (blk_i < qi) & (rank < MOBA_TOPK)
            far_row = jnp.where(keep, far_bias, NEG_BIG)
            prev_row = jnp.where(keep, 0.0, NEG_BIG)
        pieces = []
        for j in range(qi + 1):
            sj = s[j * blk:(j + 1) * blk]
            if j == qi:
                sj = sj + bown_ref[...]
            elif j == qi - 1:
                sj = sj + bprev_ref[...] + prev_row[j:j + 1]
            else:
                sj = sj + far_row[j:j + 1]
            pieces.append(sj)
        t = jnp.concatenate(pieces, axis=0) if qi else pieces[0]
        m = jnp.max(t, axis=0, keepdims=True)
        p = jnp.exp(t - m)
        l = jnp.sum(p, axis=0, keepdims=True)
        acc = jnp.dot(vtbf[:, 0:n_keys], p.astype(BF16), preferred_element_type=F32)
        o_ref[qi * blk:(qi + 1) * blk, :] = (acc / l).T.astype(o_ref.dtype)

    for c in out_copies:
        c.wait()


def _moba_prompt(proj, kv_out, layer, rel_bias, bown, bprev):
    blk = MOBA_BLOCK
    seq_col = lambda off: pl.BlockSpec((SEQ, C_DIM), lambda b, h: (b, off + h))
    in_specs = [pl.BlockSpec(memory_space=pltpu.SMEM),
                seq_col(OFF_CQ), seq_col(OFF_CK), seq_col(OFF_CV),
                pl.BlockSpec((None, blk, blk), lambda b, h: (h, 0, 0)),
                pl.BlockSpec((None, blk, blk), lambda b, h: (h, 0, 0)),
                pl.BlockSpec(memory_space=pl.ANY), pl.BlockSpec(memory_space=pl.ANY)]
    args = [rel_bias, proj, proj, proj, bown, bprev, *kv_out]
    aliases = {6: 1, 7: 2}
    kv_shape = jax.ShapeDtypeStruct((DEPTH, BATCH, SEQ, C_HEADS, C_DIM), F32)
    yc, k_out, v_out = pl.pallas_call(
        functools.partial(_moba_prompt_body, layer=layer),
        out_shape=[jax.ShapeDtypeStruct((N_TOK_P, C_WIDTH), BF16), kv_shape, kv_shape],
        grid=(BATCH, C_HEADS),
        in_specs=in_specs,
        out_specs=[pl.BlockSpec((SEQ, C_DIM), lambda b, h: (b, h)),
                   pl.BlockSpec(memory_space=pl.ANY), pl.BlockSpec(memory_space=pl.ANY)],
        scratch_shapes=[pltpu.VMEM((SEQ, C_DIM), BF16), pltpu.VMEM((C_DIM, SEQ), BF16),
                        pltpu.SemaphoreType.DMA((2,))],
        input_output_aliases=aliases,
        compiler_params=_cparams(("parallel", "parallel")),
        name="moba_prompt",
    )(*args)
    return yc, (k_out, v_out)


MEANS_BLKS = 4


def _cache_means_body(pt_ref, *refs):
    page_refs, o_ref = refs[:-1], refs[-1]
    for n in range(MEANS_BLKS):
        pages = page_refs[n * PAGES_PER_BLK:(n + 1) * PAGES_PER_BLK]
        tot = functools.reduce(lambda x, y: x + y, [jnp.sum(p[...], axis=0) for p in pages])
        o_ref[n] = tot / MOBA_BLOCK


def _cache_block_means(cache_k, pt_flat):
    def page(i):
        return pl.BlockSpec((None, None, PAGE_SIZE, C_HEADS, C_DIM),
                            lambda l, b, n, pt: (l, pt[b * N_PAGES + n * MEANS_BLKS * PAGES_PER_BLK + i], 0, 0, 0))
    n_pg = MEANS_BLKS * PAGES_PER_BLK
    return pl.pallas_call(
        _cache_means_body,
        out_shape=jax.ShapeDtypeStruct((DEPTH, DEC_BATCH, N_PAST_BLK, C_HEADS, C_DIM), F32),
        grid_spec=pltpu.PrefetchScalarGridSpec(
            num_scalar_prefetch=1,
            grid=(DEPTH, DEC_BATCH, N_PAST_BLK // MEANS_BLKS),
            in_specs=[page(i) for i in range(n_pg)],
            out_specs=pl.BlockSpec((None, None, MEANS_BLKS, C_HEADS, C_DIM),
                                   lambda l, b, n, pt: (l, b, n, 0, 0))),
        compiler_params=_cparams(("parallel", "parallel", "parallel")),
        name="cache_block_means",
    )(pt_flat, *([cache_k] * n_pg))


def _moba_gate_body(q_ref, means_ref, idx_ref):
    n_cand = DEC_BATCH * N_PAST_BLK
    q = q_ref[...]
    means = means_ref[...].reshape(n_cand, C_DIM)
    qh, ql = _split2(q)
    mh, ml = _split2(means)
    g = (lax.dot_general(qh, mh, _NT, preferred_element_type=F32)
         + lax.dot_general(qh, ml, _NT, preferred_element_type=F32)
         + lax.dot_general(ql, mh, _NT, preferred_element_type=F32))
    col = lax.broadcasted_iota(jnp.int32, (N_TOK_S, n_cand), 1)
    row = lax.broadcasted_iota(jnp.int32, (N_TOK_S, n_cand), 0)
    own_seq = (col >> int(math.log2(N_PAST_BLK))) == (row >> int(math.log2(DEC_SEQ)))
    g = jnp.where(own_seq, g, -jnp.inf)
    colf = col.astype(F32)
    lane = lax.broadcasted_iota(jnp.int32, (N_TOK_S, 128), 1)
    out = jnp.zeros((N_TOK_S, 128), jnp.int32)
    for r in range(MOBA_TOPK):
        mx = jnp.max(g, axis=1, keepdims=True)
        am = jnp.min(jnp.where(g == mx, colf, float(n_cand)), axis=1, keepdims=True)
        out = jnp.where(lane == r, am.astype(jnp.int32) & (N_PAST_BLK - 1), out)
        g = jnp.where(colf == am, -jnp.inf, g)
    idx_ref[...] = out


def _moba_gate(proj_s, means, layer):
    return pl.pallas_call(
        _moba_gate_body,
        out_shape=jax.ShapeDtypeStruct((C_HEADS, N_TOK_S, 128), jnp.int32),
        grid=(C_HEADS,),
        in_specs=[pl.BlockSpec((N_TOK_S, C_DIM), lambda h: (0, OFF_CQ + h)),
                  pl.BlockSpec((None, DEC_BATCH, N_PAST_BLK, C_DIM), lambda h: (layer, 0, 0, h))],
        out_specs=pl.BlockSpec((None, N_TOK_S, 128), lambda h: (h, 0, 0)),
        compiler_params=_cparams(("parallel",)),
        name="moba_gate",
    )(proj_s, means.reshape(DEPTH, DEC_BATCH, N_PAST_BLK, C_WIDTH))


def _moba_sample_body(idx_ref, pt_ref, rb_ref, q_ref, kn_ref, vn_ref, bs_ref, ck_ref, cv_ref, o_ref,
                      kbuf, vbuf, sem, *, layer):
    h = pl.program_id(0)
    b = pl.program_id(1)
    step = h * DEC_BATCH + b
    n_steps = C_HEADS * DEC_BATCH
    scale = C_DIM ** -0.5

    def block_of(hh, bb, s, r):
        return idx_ref[(hh * N_TOK_S + bb * DEC_SEQ + s) * MOBA_TOPK + r]

    def page_copies(st, slot):
        hh = st >> int(math.log2(DEC_BATCH))
        bb = st & (DEC_BATCH - 1)
        copies = []
        for s in range(DEC_SEQ):
            for r in range(MOBA_TOPK):
                n = block_of(hh, bb, s, r)
                for j in range(PAGES_PER_BLK):
                    pid = pt_ref[bb * N_PAGES + n * PAGES_PER_BLK + j]
                    i = (s * MOBA_TOPK + r) * PAGES_PER_BLK + j
                    copies.append(pltpu.make_async_copy(ck_ref.at[layer, pid, :, hh, :], kbuf.at[slot, i],
                                                        sem.at[0, slot]))
                    copies.append(pltpu.make_async_copy(cv_ref.at[layer, pid, :, hh, :], vbuf.at[slot, i],
                                                        sem.at[1, slot]))
        return copies

    @pl.when(step == 0)
    def _():
        for c in page_copies(step, 0):
            c.start()

    @pl.when(step + 1 < n_steps)
    def _():
        for c in page_copies(step + 1, (step + 1) & 1):
            c.start()

    slot = step & 1
    for c in page_copies(step, slot):
        c.wait()

    far_bias = rb_ref[REL_BUCKETS - 1, h]
    kn = _pad_frame(kn_ref[...]).astype(BF16)
    vn = _pad_frame(vn_ref[...]).astype(BF16)
    lane = lax.broadcasted_iota(jnp.int32, (1, FRAME), 1)
    for s in range(DEC_SEQ):
        row = b * DEC_SEQ + s
        q8 = jnp.broadcast_to(q_ref[s:s + 1, :], (8, C_DIM)).astype(BF16)
        logits = []
        for r in range(MOBA_TOPK):
            newest = block_of(h, b, s, r) == N_PAST_BLK - 1
            for j in range(PAGES_PER_BLK):
                kp = kbuf[slot, (s * MOBA_TOPK + r) * PAGES_PER_BLK + j].astype(BF16)
                lg = lax.dot_general(q8, kp, _NT, preferred_element_type=F32)[0:1] * scale
                bias = jnp.where(newest, bs_ref[s:s + 1, j * PAGE_SIZE:(j + 1) * PAGE_SIZE], far_bias)
                logits.append(lg + bias)
        lg = lax.dot_general(q8, kn, _NT, preferred_element_type=F32)[0:1] * scale
        ob = jnp.zeros((1, FRAME), F32)
        for d in range(DEC_SEQ):
            ob = jnp.where(lane == row - d, rb_ref[d, h], ob)
        valid = (lane >= b * DEC_SEQ) & (lane <= row)
        logits.append(jnp.where(valid, lg + ob, NEG_BIG))

        mx = jnp.max(functools.reduce(jnp.maximum, logits), axis=1, keepdims=True)
        ps = [jnp.exp(l - mx) for l in logits]
        denom = functools.reduce(lambda x, y: x + y, [jnp.sum(p, axis=1, keepdims=True) for p in ps])
        vals = [vbuf[slot, s * MOBA_TOPK * PAGES_PER_BLK + i].astype(BF16)
                for i in range(MOBA_TOPK * PAGES_PER_BLK)] + [vn]
        out = jnp.zeros((8, C_DIM), F32)
        for p, vv in zip(ps, vals):
            p8 = jnp.broadcast_to(p, (8, p.shape[1])).astype(BF16)
            out = out + jnp.dot(p8, vv, preferred_element_type=F32)
        o_ref[s:s + 1, :] = out[0:1] / denom


def _moba_sample(proj_s, cache_k, cache_v, idx_flat, pt_flat, rel_bias, bias_s, layer):
    n_pg = DEC_SEQ * MOBA_TOPK * PAGES_PER_BLK
    proj3 = proj_s.reshape(DEC_BATCH, DEC_SEQ, N_MAIN)
    out = pl.pallas_call(
        functools.partial(_moba_sample_body, layer=layer),
        out_shape=jax.ShapeDtypeStruct((DEC_BATCH, DEC_SEQ, C_WIDTH), F32),
        grid_spec=pltpu.PrefetchScalarGridSpec(
            num_scalar_prefetch=2,
            grid=(C_HEADS, DEC_BATCH),
            in_specs=[pl.BlockSpec(memory_space=pltpu.SMEM),
                      pl.BlockSpec((None, DEC_SEQ, C_DIM), lambda h, b, idx, pt: (b, 0, OFF_CQ + h)),
                      pl.BlockSpec((N_TOK_S, C_DIM), lambda h, b, idx, pt: (0, OFF_CK + h)),
                      pl.BlockSpec((N_TOK_S, C_DIM), lambda h, b, idx, pt: (0, OFF_CV + h)),
                      pl.BlockSpec((None, 8, MOBA_BLOCK), lambda h, b, idx, pt: (h, 0, 0)),
                      pl.BlockSpec(memory_space=pl.ANY),
                      pl.BlockSpec(memory_space=pl.ANY)],
            out_specs=pl.BlockSpec((None, DEC_SEQ, C_DIM), lambda h, b, idx, pt: (b, 0, h)),
            scratch_shapes=[pltpu.VMEM((2, n_pg, PAGE_SIZE, C_DIM), F32),
                            pltpu.VMEM((2, n_pg, PAGE_SIZE, C_DIM), F32),
                            pltpu.SemaphoreType.DMA((2, 2))]),
        compiler_params=_cparams(("arbitrary", "arbitrary")),
        name="moba_sample",
    )(idx_flat, pt_flat, rel_bias, proj3, proj_s, proj_s, bias_s, cache_k, cache_v)
    return out.reshape(N_TOK_S, C_WIDTH)


def _merge_body(ya_ref, yb_ref, yc_ref, wa_ref, wb_ref, wc_ref, g0_ref, g1_ref, g2_ref, o_ref):
    acc = g0_ref[...] * jnp.dot(ya_ref[...], wa_ref[...], preferred_element_type=F32)
    acc = acc + g1_ref[...] * jnp.dot(yb_ref[...], wb_ref[...], preferred_element_type=F32)
    acc = acc + g2_ref[...] * jnp.dot(yc_ref[...], wc_ref[...], preferred_element_type=F32)
    o_ref[...] = acc.astype(o_ref.dtype)


def _merge(ya, yb, yc, gates, w_a, w_b, w_c, layer, *, tm, tn):
    m = ya.shape[0]
    nb = D_MODEL // tn
    act = lambda width: pl.BlockSpec((tm, width), lambda j, i: (i, 0))
    wgt = lambda width: pl.BlockSpec((None, width, tn), lambda j, i: (layer, 0, j))
    gate = lambda br: pl.BlockSpec((tm, tn), lambda j, i: (i, br * nb + j))
    return pl.pallas_call(
        _merge_body,
        out_shape=jax.ShapeDtypeStruct((m, D_MODEL), BF16),
        grid=(nb, m // tm),
        in_specs=[act(A_WIDTH), act(B_WIDTH), act(C_WIDTH), wgt(A_WIDTH), wgt(B_WIDTH), wgt(C_WIDTH),
                  gate(0), gate(1), gate(2)],
        out_specs=pl.BlockSpec((tm, tn), lambda j, i: (i, j)),
        compiler_params=_cparams(("parallel", "parallel")),
        name="branch_merge",
    )(ya, yb, yc, w_a, w_b, w_c, gates, gates, gates)


def kernel(x_prompt, x_sample, cache_k, cache_v, state_hgrn, state_conv, page_table, w_in, b_gate,
           conv_w, hgrn_lb_logits, hgrn_norm_g, w_a_up, w_b_up, w_c_up, w_o, rel_bias, g_pre_mix,
           g_post_mix, g_pre_mlp, g_post_mlp, w_mlp_up, w_mlp_down):
    consts = {k: jnp.asarray(v) for k, v in _constants().items()}
    for name in ("m3_p", "emask_p", "m3_s", "emask_s"):
        consts[name] = consts[name].astype(BF16)

    w_in_b, w_a_b, w_b_b, w_c_b = (w.astype(BF16) for w in (w_in, w_a_up, w_b_up, w_c_up))
    w_o_b, w_up_b, w_dn_b = (w.astype(BF16) for w in (w_o, w_mlp_up, w_mlp_down))
    vec = lambda g: g.reshape(DEPTH, 1, g.shape[-1])
    b_gate3, ng3 = vec(b_gate), vec(hgrn_norm_g)
    gpm, gqm, gpl, gql = vec(g_pre_mix), vec(g_post_mix), vec(g_pre_mlp), vec(g_post_mlp)
    pt_flat = page_table.reshape(-1)

    lbp = _lower_bound_params(hgrn_lb_logits)
    bown, bprev, bias_s = _bias_tables(rel_bias, consts)
    means = _cache_block_means(cache_k, pt_flat)

    groups = {
        "p": dict(x=x_prompt.reshape(N_TOK_P, D_MODEL), tm_row=128, mm=dict(tm=1024, tn=1024, tk=D_MODEL),
                  mm_down=dict(tm=2048, tn=1024, tk=2048), merge=dict(tm=512, tn=1024)),
        "s": dict(x=x_sample.reshape(N_TOK_S, D_MODEL), tm_row=N_TOK_S, mm=dict(tm=N_TOK_S, tn=1024, tk=D_MODEL),
                  mm_down=dict(tm=N_TOK_S, tn=1024, tk=D_MODEL), merge=dict(tm=N_TOK_S, tn=1024)),
    }
    for g in groups.values():
        g["h"] = _prenorm(g["x"], gpm, 0, tm=g["tm_row"])
    outs = {k: [] for k in ("ks", "vs", "sp", "ss", "cp", "cs")}
    kv_shape_p = (DEPTH, BATCH, SEQ, C_HEADS, C_DIM)
    kv_p = (jnp.zeros(kv_shape_p, F32), jnp.zeros(kv_shape_p, F32))

    for l in range(DEPTH):
        for tag, g in groups.items():
            mm = g["mm"]
            proj = _matmul(g["h"], w_in_b, l, col0=0, n_cols=N_MAIN, name="in_proj", **mm)
            gates = _matmul(g["h"], w_in_b, l, col0=N_MAIN, n_cols=N_GATE, epilogue="sigmoid_bias",
                            bias=b_gate3, name="in_proj_gates", **mm)
            if tag == "p":
                ya, st = _hgrn_prompt(proj, lbp, ng3, l, consts)
                yb, cn = _shortconv(proj.reshape(BATCH, SEQ, N_MAIN), conv_w, l, None, tc=256, out_dtype=BF16)
                yb = yb.reshape(N_TOK_P, B_WIDTH)
                yc, kv_p = _moba_prompt(proj, kv_p, l, rel_bias, bown, bprev)
            else:
                ya, st = _hgrn_sample(proj, state_hgrn, lbp, ng3, l, consts)
                yb, cn = _shortconv(proj.reshape(DEC_BATCH, DEC_SEQ, N_MAIN), conv_w, l, state_conv,
                                    tc=B_WIDTH, out_dtype=F32)
                yb = yb.reshape(N_TOK_S, B_WIDTH).astype(BF16)
                idx = _moba_gate(proj, means, l)
                idx_flat = idx[:, :, :MOBA_TOPK].reshape(-1)
                yc = _moba_sample(proj, cache_k, cache_v, idx_flat, pt_flat, rel_bias, bias_s, l).astype(BF16)
                kv_shape = (DEC_BATCH, DEC_SEQ, C_HEADS, C_DIM)
                outs["ks"].append(proj[:, OFF_CK * 128:OFF_CK * 128 + C_WIDTH].reshape(kv_shape))
                outs["vs"].append(proj[:, OFF_CV * 128:OFF_CV * 128 + C_WIDTH].reshape(kv_shape))
            outs["s" + tag].append(st)
            outs["c" + tag].append(cn)

            merged = _merge(ya, yb, yc, gates, w_a_b, w_b_b, w_c_b, l, **g["merge"])
            mix = _matmul(merged, w_o_b, l, name="out_proj", **mm)
            x, hm = _post(g["x"], mix, gqm, l, gpl, l, tm=g["tm_row"])
            up = _matmul(hm, w_up_b, l, epilogue="relu2", out_dtype=BF16, name="mlp_up", **mm)
            dn = _matmul(up, w_dn_b, l, name="mlp_down", **g["mm_down"])
            if l + 1 < DEPTH:
                g["x"], g["h"] = _post(x, dn, gql, l, gpm, l + 1, tm=g["tm_row"])
            else:
                g["x"], g["h"] = _post(x, dn, gql, l, tm=g["tm_row"])

    return (groups["p"]["x"].reshape(BATCH, SEQ, D_MODEL),
            groups["s"]["x"].reshape(DEC_BATCH, DEC_SEQ, D_MODEL),
            kv_p[0], kv_p[1], jnp.stack(outs["ks"]), jnp.stack(outs["vs"]),
            jnp.stack(outs["sp"]), jnp.stack(outs["ss"]), jnp.stack(outs["cp"]), jnp.stack(outs["cs"]))
```

```python
import functools
import math

import numpy as np
import jax
import jax.numpy as jnp
from jax import lax
from jax.experimental import pallas as pl
from jax.experimental.pallas import tpu as pltpu

F32 = jnp.float32
BF16 = jnp.bfloat16

D_MODEL = 4096
BATCH = 4
SEQ = 2048
DEPTH = 4
DEC_BATCH = 8
DEC_SEQ = 4
PAST_LEN = 8192
PAGE_SIZE = 128
A_HEADS = 8
A_DIM = 128
A_WIDTH = A_HEADS * A_DIM
A_CHUNK = 16
B_WIDTH = D_MODEL // 4
C_HEADS = 16
C_DIM = 128
C_WIDTH = C_HEADS * C_DIM
MOBA_BLOCK = 256
MOBA_TOPK = 3
REL_BUCKETS = 32
REL_MAX_DIST = 128
D_FF = 4 * D_MODEL
NORM_EPS = 1e-6
NEG_BIG = -1e30
LB_FLOOR = 1e-30

N_TOK_P = BATCH * SEQ
N_TOK_S = DEC_BATCH * DEC_SEQ
N_MAIN = 4 * A_WIDTH + 3 * B_WIDTH + 3 * C_WIDTH
N_GATE = 3 * D_MODEL
OFF_AQ, OFF_AF, OFF_AI, OFF_AG = 0, 8, 16, 24
OFF_BB, OFF_BC, OFF_BX = 32, 40, 48
OFF_CQ, OFF_CK, OFF_CV = 56, 72, 88
N_PAGES = PAST_LEN // PAGE_SIZE
N_PAST_BLK = PAST_LEN // MOBA_BLOCK
PAGES_PER_BLK = MOBA_BLOCK // PAGE_SIZE
FRAME = 128
HGRN_UNROLL = 8

V7X_VMEM_LIMIT = 48 * 1024 * 1024
V7X_VMEM_BIG = 56 * 1024 * 1024

_NT = (((1,), (1,)), ((), ()))


def _cparams(sem, vmem=V7X_VMEM_LIMIT):
    return pltpu.CompilerParams(dimension_semantics=sem, vmem_limit_bytes=vmem)


def _sigmoid(x):
    return 1.0 / (1.0 + jnp.exp(-x))


def _split3(x):
    hi = x.astype(BF16)
    r1 = x - hi.astype(F32)
    lo = r1.astype(BF16)
    lo2 = (r1 - lo.astype(F32)).astype(BF16)
    return hi, lo, lo2


def _split2(x):
    hi = x.astype(BF16)
    lo = (x - hi.astype(F32)).astype(BF16)
    return hi, lo


def _rel_bucket_np(rel):
    n = np.maximum(rel, 0)
    max_exact = REL_BUCKETS // 2
    nf = np.maximum(n, max_exact).astype(np.float32)
    large = max_exact + (np.log(nf / max_exact) / math.log(REL_MAX_DIST / max_exact)
                         * (REL_BUCKETS - max_exact)).astype(np.int32)
    large = np.clip(large, max_exact, REL_BUCKETS - 1)
    return np.where(n < max_exact, n, large).astype(np.int32)


def _chunk_mats(n, chunk):
    r = np.arange(n)[:, None]
    c = np.arange(n)[None, :]
    same = (r // chunk) == (c // chunk)
    t_cum = same & (c <= r)
    t_last = same
    t_ref = same & ((c % chunk) <= chunk // 2)
    m3 = np.concatenate([t_cum, t_last, t_ref], axis=0).astype(np.float32)
    return m3, t_cum.astype(np.float32)


@functools.lru_cache(maxsize=None)
def _constants():
    m3_p, causal_p = _chunk_mats(MOBA_BLOCK, A_CHUNK)
    rows = np.arange(16 * A_DIM)[:, None]
    cols = np.arange(256)[None, :]
    emask_p = ((rows // A_DIM) == (cols // A_CHUNK)).astype(np.float32)
    m3_s, causal_s = _chunk_mats(FRAME, DEC_SEQ)
    rows = np.arange(DEC_BATCH * A_DIM)[:, None]
    cols = np.arange(FRAME)[None, :]
    emask_s = (((rows // A_DIM) == (cols // DEC_SEQ)) & (cols < N_TOK_S)).astype(np.float32)
    kl = np.arange(MOBA_BLOCK)[:, None]
    ql = np.arange(MOBA_BLOCK)[None, :]
    bucket_own = _rel_bucket_np(ql - kl)
    bucket_prev = _rel_bucket_np(MOBA_BLOCK + ql - kl)
    s = np.arange(8)[:, None]
    t = np.arange(MOBA_BLOCK)[None, :]
    bucket_s = _rel_bucket_np(MOBA_BLOCK + s - t)
    return dict(m3_p=m3_p, causal_p=causal_p, emask_p=emask_p, m3_s=m3_s, causal_s=causal_s,
                emask_s=emask_s, bucket_own=bucket_own, bucket_prev=bucket_prev, bucket_s=bucket_s)


def _mm_body(*refs, nk, epilogue, tm):
    refs = list(refs)
    a_ref, w_ref = refs[:2]
    b_ref = refs[2] if epilogue == "sigmoid_bias" else None
    as_ref, o_ref, os_ref, acat = refs[-4:]
    ms = as_ref.shape[0]
    i = pl.program_id(0)

    def epi(r):
        if epilogue == "sigmoid_bias":
            r = _sigmoid(r + b_ref[...])
        elif epilogue == "relu2":
            r = jnp.square(jnp.maximum(r, 0.0))
        return r

    def both_groups():
        acat[0:tm, :] = a_ref[...]
        acat[tm:tm + ms, :] = as_ref[...]
        return jnp.dot(acat[...], w_ref[...], preferred_element_type=F32)

    if nk == 1:
        @pl.when(i == 0)
        def _():
            r = both_groups()
            o_ref[...] = epi(r[0:tm]).astype(o_ref.dtype)
            os_ref[...] = epi(r[tm:tm + ms]).astype(os_ref.dtype)

        @pl.when(i > 0)
        def _():
            o_ref[...] = epi(jnp.dot(a_ref[...], w_ref[...], preferred_element_type=F32)).astype(o_ref.dtype)
        return

    k = pl.program_id(2)

    @pl.when(k == 0)
    def _():
        o_ref[...] = jnp.zeros(o_ref.shape, F32)

    @pl.when((k == 0) & (i == 0))
    def _():
        os_ref[...] = jnp.zeros(os_ref.shape, F32)

    @pl.when(i == 0)
    def _():
        r = both_groups()
        o_ref[...] += r[0:tm]
        os_ref[...] += r[tm:tm + ms]

    @pl.when(i > 0)
    def _():
        o_ref[...] += jnp.dot(a_ref[...], w_ref[...], preferred_element_type=F32)


def _matmul(a, a_s, w, layer, *, col0=0, n_cols=None, tm, tn, tk, epilogue="none", bias=None,
            out_dtype=F32, name):
    m, kdim = a.shape
    ms = a_s.shape[0]
    n_cols = w.shape[2] if n_cols is None else n_cols
    assert m % tm == 0 and kdim % tk == 0 and n_cols % tn == 0 and col0 % tn == 0
    nk = kdim // tk
    nj = n_cols // tn
    assert nk == 1 or (epilogue == "none" and out_dtype == F32)
    cb = col0 // tn
    in_specs = [pl.BlockSpec((tm, tk), lambda i, j, k: (i, k)),
                pl.BlockSpec((None, tk, tn), lambda i, j, k: (layer, k, j + cb))]
    args = [a, w]
    if epilogue == "sigmoid_bias":
        in_specs.append(pl.BlockSpec((None, 1, tn), lambda i, j, k: (layer, 0, j)))
        args.append(bias)
    in_specs.append(pl.BlockSpec((ms, tk), lambda i, j, k: (0, k)))
    args.append(a_s)
    out_specs = [pl.BlockSpec((tm, tn), lambda i, j, k: (i, j)),
                 pl.BlockSpec((ms, tn), lambda i, j, k: (0, jnp.where(i == 0, j, nj - 1)))]
    return pl.pallas_call(
        functools.partial(_mm_body, nk=nk, epilogue=epilogue, tm=tm),
        out_shape=[jax.ShapeDtypeStruct((m, n_cols), out_dtype), jax.ShapeDtypeStruct((ms, n_cols), out_dtype)],
        grid=(m // tm, nj, nk),
        in_specs=in_specs,
        out_specs=out_specs,
        scratch_shapes=[pltpu.VMEM((tm + ms, tk), BF16)],
        compiler_params=_cparams(("arbitrary", "arbitrary", "arbitrary"), V7X_VMEM_BIG),
        name=name,
    )(*args)


def _rms(x, g):
    return x * lax.rsqrt(jnp.mean(x * x, axis=-1, keepdims=True) + NORM_EPS) * g


def _prenorm_body(x_ref, g_ref, h_ref):
    h_ref[...] = _rms(x_ref[...], g_ref[...]).astype(h_ref.dtype)


def _prenorm(x, g, layer, *, tm):
    m = x.shape[0]
    return pl.pallas_call(
        _prenorm_body,
        out_shape=jax.ShapeDtypeStruct((m, D_MODEL), BF16),
        grid=(m // tm,),
        in_specs=[pl.BlockSpec((tm, D_MODEL), lambda i: (i, 0)),
                  pl.BlockSpec((None, 1, D_MODEL), lambda i: (layer, 0, 0))],
        out_specs=pl.BlockSpec((tm, D_MODEL), lambda i: (i, 0)),
        compiler_params=_cparams(("parallel",)),
        name="prenorm",
    )(x, g)


def _post_body(x_ref, y_ref, gpost_ref, *rest, with_next):
    xn = x_ref[...] + _rms(y_ref[...], gpost_ref[...])
    if with_next:
        gnext_ref, xo_ref, ho_ref = rest
        ho_ref[...] = _rms(xn, gnext_ref[...]).astype(ho_ref.dtype)
    else:
        (xo_ref,) = rest
    xo_ref[...] = xn


def _post(x, y, gpost, layer, gnext=None, next_layer=None, *, tm):
    m = x.shape[0]
    with_next = gnext is not None
    row = pl.BlockSpec((tm, D_MODEL), lambda i: (i, 0))
    in_specs = [row, row, pl.BlockSpec((None, 1, D_MODEL), lambda i: (layer, 0, 0))]
    args = [x, y, gpost]
    out_shape = [jax.ShapeDtypeStruct((m, D_MODEL), F32)]
    out_specs = [row]
    if with_next:
        in_specs.append(pl.BlockSpec((None, 1, D_MODEL), lambda i: (next_layer, 0, 0)))
        args.append(gnext)
        out_shape.append(jax.ShapeDtypeStruct((m, D_MODEL), BF16))
        out_specs.append(row)
    res = pl.pallas_call(
        functools.partial(_post_body, with_next=with_next),
        out_shape=out_shape,
        grid=(m // tm,),
        in_specs=in_specs,
        out_specs=out_specs,
        compiler_params=_cparams(("parallel",)),
        name="post_norm",
    )(*args)
    return (res[0], res[1]) if with_next else (res[0], None)


def _lb_body(x_ref, o_ref):
    x = x_ref[...]
    rows = [x[l:l + 1] for l in range(DEPTH)]
    mx = functools.reduce(jnp.maximum, rows)
    es = [jnp.exp(r - mx) for r in rows]
    tot = functools.reduce(lambda a, b: a + b, es)
    ps = [e / tot for e in es]
    cum = None
    for l in range(DEPTH):
        cum = ps[l] if cum is None else cum + ps[l]
        lb = jnp.clip(cum - ps[0], 0.0, 1.0)
        o_ref[3 * l:3 * l + 1, :] = jnp.log(jnp.maximum(lb, LB_FLOOR))
        o_ref[3 * l + 1:3 * l + 2, :] = jnp.log1p(-lb)
        o_ref[3 * l + 2:3 * l + 3, :] = 1.0 - lb


def _lower_bound_params(lb_logits):
    out = pl.pallas_call(
        _lb_body,
        out_shape=jax.ShapeDtypeStruct((3 * DEPTH, A_WIDTH), F32),
        name="hgrn_lower_bounds",
    )(lb_logits)
    return out.reshape(DEPTH, 3, A_HEADS, 1, A_DIM)


def _hgrn_gates(aq, z, la, l1m, oml):
    q = aq * _sigmoid(aq)
    log_sig = jnp.minimum(z, 0.0) - jnp.log1p(jnp.exp(-jnp.abs(z)))
    b = l1m + log_sig
    log_f = jnp.maximum(la, b) + jnp.log1p(jnp.exp(-jnp.abs(la - b)))
    k = oml / (1.0 + jnp.exp(z))
    return q, k, log_f


def _chunk_sums(m3, log_f, n):
    hi, lo, lo2 = _split3(log_f)
    c3 = (jnp.dot(m3, hi, preferred_element_type=F32)
          + jnp.dot(m3, lo, preferred_element_type=F32)
          + jnp.dot(m3, lo2, preferred_element_type=F32))
    return c3[0:n], c3[n:2 * n], c3[2 * n:3 * n]


def _hgrn_prompt_body(aq_ref, af_ref, ai_ref, ag_ref, lbp_ref, ng_ref, m3_ref, cmask_ref, emask_ref,
                      ya_ref, st_ref, o_scr, qc_scr, dec_scr, ut_scr):
    blk = MOBA_BLOCK
    n_blk = SEQ // blk
    cpb = blk // A_CHUNK
    la = lbp_ref[0]
    l1m = lbp_ref[1]
    oml = lbp_ref[2]

    def block_body(j, carry):
        rows = pl.ds(pl.multiple_of(j * blk, blk), blk)
        q, k, log_f = _hgrn_gates(aq_ref[rows, :], af_ref[rows, :], la, l1m, oml)
        v = ai_ref[rows, :]
        cum, last_b, ref_b = _chunk_sums(m3_ref[...], log_f, blk)
        q_rel = (q * jnp.exp(cum - ref_b)).astype(BF16)
        k_rel = (k * jnp.exp(ref_b - cum)).astype(BF16)
        k_last = (k * jnp.exp(last_b - cum)).astype(BF16)
        a = lax.dot_general(q_rel, k_rel, _NT, preferred_element_type=F32)
        a = jnp.where(cmask_ref[...] > 0.0, a, 0.0)
        o_scr[rows, :] = jnp.dot(a.astype(BF16), v.astype(BF16), preferred_element_type=F32)
        qc_scr[rows, :] = (q * jnp.exp(cum)).astype(BF16)
        dec_scr[rows, :] = jnp.exp(last_b)
        vt = v.T.astype(BF16)
        vt_exp = jnp.concatenate([vt] * cpb, axis=0) * emask_ref[...]
        ut_rows = pl.ds(pl.multiple_of(j * (cpb * A_DIM), cpb * A_DIM), cpb * A_DIM)
        ut_scr[ut_rows, :] = jnp.dot(vt_exp, k_last, preferred_element_type=F32)
        return carry

    lax.fori_loop(0, n_blk, block_body, 0, unroll=HGRN_UNROLL)

    def scan_body(n, st):
        rows = pl.ds(pl.multiple_of(n * A_DIM, A_DIM), A_DIM)
        u = ut_scr[rows, :]
        ut_scr[rows, :] = st
        d = dec_scr[pl.ds(pl.multiple_of(n * A_CHUNK, A_CHUNK), 8), :][0:1, :]
        return st * d + u

    st = lax.fori_loop(0, SEQ // A_CHUNK, scan_body, jnp.zeros((A_DIM, A_DIM), F32), unroll=4)
    st_ref[...] = st.T

    ng = ng_ref[...]

    def out_body(j, carry):
        rows = pl.ds(pl.multiple_of(j * blk, blk), blk)
        ut_rows = pl.ds(pl.multiple_of(j * (cpb * A_DIM), cpb * A_DIM), cpb * A_DIM)
        starts = ut_scr[ut_rows, :].astype(BF16)
        r = lax.dot_general(qc_scr[rows, :], starts, _NT, preferred_element_type=F32)
        o_inter = jnp.concatenate(
            [r[n * A_CHUNK:(n + 1) * A_CHUNK, n * A_DIM:(n + 1) * A_DIM] for n in range(cpb)], axis=0)
        o = _rms(o_scr[rows, :] + o_inter, ng)
        ag = ag_ref[rows, :]
        ya_ref[rows, :] = (o * (ag * _sigmoid(ag))).astype(ya_ref.dtype)
        return carry

    lax.fori_loop(0, n_blk, out_body, 0, unroll=HGRN_UNROLL)


def _hgrn_prompt(proj, lbp, norm_g, layer, consts):
    col = lambda off: pl.BlockSpec((SEQ, A_DIM), lambda b, h: (b, off + h))
    whole = lambda shape: pl.BlockSpec(shape, lambda b, h: (0,) * len(shape))
    return pl.pallas_call(
        _hgrn_prompt_body,
        out_shape=[jax.ShapeDtypeStruct((N_TOK_P, A_WIDTH), BF16),
                   jax.ShapeDtypeStruct((BATCH, A_HEADS, A_DIM, A_DIM), F32)],
        grid=(BATCH, A_HEADS),
        in_specs=[col(OFF_AQ), col(OFF_AF), col(OFF_AI), col(OFF_AG),
                  pl.BlockSpec((None, 3, None, 1, A_DIM), lambda b, h: (layer, 0, h, 0, 0)),
                  pl.BlockSpec((None, 1, A_DIM), lambda b, h: (layer, 0, 0)),
                  whole((3 * MOBA_BLOCK, MOBA_BLOCK)), whole((MOBA_BLOCK, MOBA_BLOCK)),
                  whole((16 * A_DIM, MOBA_BLOCK))],
        out_specs=[pl.BlockSpec((SEQ, A_DIM), lambda b, h: (b, h)),
                   pl.BlockSpec((None, None, A_DIM, A_DIM), lambda b, h: (b, h, 0, 0))],
        scratch_shapes=[pltpu.VMEM((SEQ, A_DIM), F32), pltpu.VMEM((SEQ, A_DIM), BF16),
                        pltpu.VMEM((SEQ, A_DIM), F32), pltpu.VMEM((SEQ // A_CHUNK * A_DIM, A_DIM), F32)],
        compiler_params=_cparams(("parallel", "parallel")),
        name="hgrn_prompt",
    )(proj, proj, proj, proj, lbp, norm_g, consts["m3_p"], consts["causal_p"], consts["emask_p"])


def _pad_frame(x):
    return jnp.concatenate([x, jnp.zeros((FRAME - x.shape[0], x.shape[1]), x.dtype)], axis=0)


def _hgrn_sample_body(aq_ref, af_ref, ai_ref, ag_ref, s0_ref, lbp_ref, ng_ref, m3_ref, cmask_ref,
                      emask_ref, ya_ref, sn_ref):
    la = lbp_ref[0]
    l1m = lbp_ref[1]
    oml = lbp_ref[2]
    q, k, log_f = _hgrn_gates(_pad_frame(aq_ref[...]), _pad_frame(af_ref[...]), la, l1m, oml)
    v = _pad_frame(ai_ref[...])
    cum, last_b, ref_b = _chunk_sums(m3_ref[...], log_f, FRAME)
    q_rel = (q * jnp.exp(cum - ref_b)).astype(BF16)
    k_rel = (k * jnp.exp(ref_b - cum)).astype(BF16)
    k_last = (k * jnp.exp(last_b - cum)).astype(BF16)
    a = lax.dot_general(q_rel, k_rel, _NT, preferred_element_type=F32)
    a = jnp.where(cmask_ref[...] > 0.0, a, 0.0)
    o_intra = jnp.dot(a.astype(BF16), v.astype(BF16), preferred_element_type=F32)
    qc = (q * jnp.exp(cum)).astype(BF16)
    dec = jnp.exp(last_b)
    vt = v.T.astype(BF16)
    vt_exp = jnp.concatenate([vt] * DEC_BATCH, axis=0) * emask_ref[...]
    ut = jnp.dot(vt_exp, k_last, preferred_element_type=F32)
    s0t = jnp.concatenate([s0_ref[b].T for b in range(DEC_BATCH)], axis=0)
    r = lax.dot_general(qc, s0t.astype(BF16), _NT, preferred_element_type=F32)
    row_b = lax.broadcasted_iota(jnp.int32, (FRAME, A_DIM), 0) >> int(math.log2(DEC_SEQ))
    o_inter = jnp.zeros((FRAME, A_DIM), F32)
    for b in range(DEC_BATCH):
        o_inter = jnp.where(row_b == b, r[:, b * A_DIM:(b + 1) * A_DIM], o_inter)
        d = dec[b * DEC_SEQ:b * DEC_SEQ + 1, :]
        snt = s0t[b * A_DIM:(b + 1) * A_DIM, :] * d + ut[b * A_DIM:(b + 1) * A_DIM, :]
        sn_ref[b] = snt.T
    o = _rms(o_intra + o_inter, ng_ref[...])
    ag = _pad_frame(ag_ref[...])
    ya_ref[...] = (o * (ag * _sigmoid(ag)))[0:N_TOK_S].astype(ya_ref.dtype)


def _hgrn_sample(proj, state, lbp, norm_g, layer, consts):
    col = lambda off: pl.BlockSpec((N_TOK_S, A_DIM), lambda h: (0, off + h))
    whole = lambda shape: pl.BlockSpec(shape, lambda h: (0,) * len(shape))
    return pl.pallas_call(
        _hgrn_sample_body,
        out_shape=[jax.ShapeDtypeStruct((N_TOK_S, A_WIDTH), BF16),
                   jax.ShapeDtypeStruct((DEC_BATCH, A_HEADS, A_DIM, A_DIM), F32)],
        grid=(A_HEADS,),
        in_specs=[col(OFF_AQ), col(OFF_AF), col(OFF_AI), col(OFF_AG),
                  pl.BlockSpec((None, DEC_BATCH, None, A_DIM, A_DIM), lambda h: (layer, 0, h, 0, 0)),
                  pl.BlockSpec((None, 3, None, 1, A_DIM), lambda h: (layer, 0, h, 0, 0)),
                  pl.BlockSpec((None, 1, A_DIM), lambda h: (layer, 0, 0)),
                  whole((3 * FRAME, FRAME)), whole((FRAME, FRAME)), whole((DEC_BATCH * A_DIM, FRAME))],
        out_specs=[pl.BlockSpec((N_TOK_S, A_DIM), lambda h: (0, h)),
                   pl.BlockSpec((DEC_BATCH, None, A_DIM, A_DIM), lambda h: (0, h, 0, 0))],
        compiler_params=_cparams(("parallel",)),
        name="hgrn_sample",
    )(proj, proj, proj, proj, state, lbp, norm_g, consts["m3_s"], consts["causal_s"], consts["emask_s"])


def _conv_body(bb_ref, bc_ref, bx_ref, w_ref, *rest, t, has_prev):
    if has_prev:
        prev_ref, y_ref, cn_ref, scr = rest
    else:
        y_ref, cn_ref, scr = rest
    u = bc_ref[...] * bx_ref[...]
    scr[0:8, :] = jnp.zeros((8, scr.shape[1]), F32)
    if has_prev:
        scr[6:8, :] = prev_ref[...]
    scr[8:8 + t, :] = u
    w = w_ref[...]
    y = w[0:1] * scr[6:6 + t, :] + w[1:2] * scr[7:7 + t, :] + w[2:3] * u
    y_ref[...] = (bb_ref[...] * y).astype(y_ref.dtype)
    cn_ref[...] = scr[6 + t:8 + t, :]


def _shortconv(proj3, conv_w, layer, prev, *, tc, out_dtype):
    bsz, t, _ = proj3.shape
    has_prev = prev is not None
    lanes = lambda off: pl.BlockSpec((None, t, tc), lambda b, c: (b, 0, off * 128 // tc + c))
    in_specs = [lanes(OFF_BB), lanes(OFF_BC), lanes(OFF_BX),
                pl.BlockSpec((None, 3, tc), lambda b, c: (layer, 0, c))]
    args = [proj3, proj3, proj3, conv_w]
    if has_prev:
        in_specs.append(pl.BlockSpec((None, None, 2, tc), lambda b, c: (layer, b, 0, c)))
        args.append(prev)
    return pl.pallas_call(
        functools.partial(_conv_body, t=t, has_prev=has_prev),
        out_shape=[jax.ShapeDtypeStruct((bsz, t, B_WIDTH), out_dtype),
                   jax.ShapeDtypeStruct((bsz, 2, B_WIDTH), F32)],
        grid=(bsz, B_WIDTH // tc),
        in_specs=in_specs,
        out_specs=[pl.BlockSpec((None, t, tc), lambda b, c: (b, 0, c)),
                   pl.BlockSpec((None, 2, tc), lambda b, c: (b, 0, c))],
        scratch_shapes=[pltpu.VMEM((t + 8, tc), F32)],
        compiler_params=_cparams(("parallel", "parallel")),
        name="shortconv",
    )(*args)


def _bias_body(rb_ref, bo_ref, bp_ref, bs_ref, own_ref, prev_ref, smp_ref):
    h = pl.program_id(0)

    def lookup(bucket):
        acc = jnp.zeros(bucket.shape, F32)
        for b in range(REL_BUCKETS):
            acc = jnp.where(bucket == b, rb_ref[b, h], acc)
        return acc

    blk = MOBA_BLOCK
    key_i = lax.broadcasted_iota(jnp.int32, (blk, blk), 0)
    qry_i = lax.broadcasted_iota(jnp.int32, (blk, blk), 1)
    own_ref[...] = jnp.where(key_i <= qry_i, lookup(bo_ref[...]), NEG_BIG)
    prev_ref[...] = lookup(bp_ref[...])
    smp_ref[...] = lookup(bs_ref[...])


def _bias_tables(rel_bias, consts):
    whole = lambda shape: pl.BlockSpec(shape, lambda h: (0,) * len(shape))
    blk = MOBA_BLOCK
    return pl.pallas_call(
        _bias_body,
        out_shape=[jax.ShapeDtypeStruct((C_HEADS, blk, blk), F32),
                   jax.ShapeDtypeStruct((C_HEADS, blk, blk), F32),
                   jax.ShapeDtypeStruct((C_HEADS, 8, blk), F32)],
        grid=(C_HEADS,),
        in_specs=[pl.BlockSpec(memory_space=pltpu.SMEM),
                  whole((blk, blk)), whole((blk, blk)), whole((8, blk))],
        out_specs=[pl.BlockSpec((None, blk, blk), lambda h: (h, 0, 0)),
                   pl.BlockSpec((None, blk, blk), lambda h: (h, 0, 0)),
                   pl.BlockSpec((None, 8, blk), lambda h: (h, 0, 0))],
        compiler_params=_cparams(("parallel",)),
        name="rel_bias_tables",
    )(rel_bias, consts["bucket_own"], consts["bucket_prev"], consts["bucket_s"])


def _moba_prompt_body(rb_ref, q_ref, k_ref, v_ref, bown_ref, bprev_ref, *rest, layer):
    o_ref, ko_ref, vo_ref, kbf, vtbf, sem = rest[-6:]
    blk = MOBA_BLOCK
    n_blk = SEQ // blk
    b = pl.program_id(0)
    h = pl.program_id(1)
    scale = C_DIM ** -0.5
    far_bias = rb_ref[REL_BUCKETS - 1, h]

    out_copies = [pltpu.make_async_copy(k_ref, ko_ref.at[layer, b, :, h, :], sem.at[0]),
                  pltpu.make_async_copy(v_ref, vo_ref.at[layer, b, :, h, :], sem.at[1])]
    for c in out_copies:
        c.start()

    means = []
    for i in range(n_blk):
        kb = k_ref[i * blk:(i + 1) * blk, :]
        kbf[i * blk:(i + 1) * blk, :] = kb.astype(BF16)
        means.append(jnp.sum(kb, axis=0, keepdims=True) / blk)
        vtbf[:, i * blk:(i + 1) * blk] = v_ref[i * blk:(i + 1) * blk, :].T.astype(BF16)
    means = jnp.concatenate(means + [jnp.zeros((n_blk, C_DIM), F32)], axis=0)
    mh, ml = _split2(means)
    blk_i = lax.broadcasted_iota(jnp.int32, (n_blk, blk), 0)

    for qi in range(n_blk):
        q = q_ref[qi * blk:(qi + 1) * blk, :]
        n_keys = (qi + 1) * blk
        s = lax.dot_general(kbf[0:n_keys, :], q.astype(BF16), _NT, preferred_element_type=F32) * scale
        if qi >= 1:
            qh, ql = _split2(q)
            gate = (lax.dot_general(mh, qh, _NT, preferred_element_type=F32)
                    + lax.dot_general(mh, ql, _NT, preferred_element_type=F32)
                    + lax.dot_general(ml, qh, _NT, preferred_element_type=F32))[0:n_blk]
            rank = jnp.zeros((n_blk, blk), jnp.int32)
            for jp in range(qi):
                gj = gate[jp:jp + 1, :]
                beats = (gj > gate) | ((gj == gate) & (jp < blk_i))
                rank = rank + jnp.where(beats, 1, 0)
            keep = (blk_i < qi) & (rank < MOBA_TOPK)
            far_row = jnp.where(keep, far_bias, NEG_BIG)
            prev_row = jnp.where(keep, 0.0, NEG_BIG)
        pieces = []
        for j in range(qi + 1):
            sj = s[j * blk:(j + 1) * blk]
            if j == qi:
                sj = sj + bown_ref[...]
            elif j == qi - 1:
                sj = sj + bprev_ref[...] + prev_row[j:j + 1]
            else:
                sj = sj + far_row[j:j + 1]
            pieces.append(sj)
        t = jnp.concatenate(pieces, axis=0) if qi else pieces[0]
        m = jnp.max(t, axis=0, keepdims=True)
        p = jnp.exp(t - m)
        l = jnp.sum(p, axis=0, keepdims=True)
        acc = jnp.dot(vtbf[:, 0:n_keys], p.astype(BF16), preferred_element_type=F32)
        o_ref[qi * blk:(qi + 1) * blk, :] = (acc / l).T.astype(o_ref.dtype)

    for c in out_copies:
        c.wait()


def _moba_prompt(proj, kv_out, layer, rel_bias, bown, bprev):
    blk = MOBA_BLOCK
    seq_col = lambda off: pl.BlockSpec((SEQ, C_DIM), lambda b, h: (b, off + h))
    in_specs = [pl.BlockSpec(memory_space=pltpu.SMEM),
                seq_col(OFF_CQ), seq_col(OFF_CK), seq_col(OFF_CV),
                pl.BlockSpec((None, blk, blk), lambda b, h: (h, 0, 0)),
                pl.BlockSpec((None, blk, blk), lambda b, h: (h, 0, 0)),
                pl.BlockSpec(memory_space=pl.ANY), pl.BlockSpec(memory_space=pl.ANY)]
    args = [rel_bias, proj, proj, proj, bown, bprev, *kv_out]
    aliases = {6: 1, 7: 2}
    kv_shape = jax.ShapeDtypeStruct((DEPTH, BATCH, SEQ, C_HEADS, C_DIM), F32)
    yc, k_out, v_out = pl.pallas_call(
        functools.partial(_moba_prompt_body, layer=layer),
        out_shape=[jax.ShapeDtypeStruct((N_TOK_P, C_WIDTH), BF16), kv_shape, kv_shape],
        grid=(BATCH, C_HEADS),
        in_specs=in_specs,
        out_specs=[pl.BlockSpec((SEQ, C_DIM), lambda b, h: (b, h)),
                   pl.BlockSpec(memory_space=pl.ANY), pl.BlockSpec(memory_space=pl.ANY)],
        scratch_shapes=[pltpu.VMEM((SEQ, C_DIM), BF16), pltpu.VMEM((C_DIM, SEQ), BF16),
                        pltpu.SemaphoreType.DMA((2,))],
        input_output_aliases=aliases,
        compiler_params=_cparams(("parallel", "parallel")),
        name="moba_prompt",
    )(*args)
    return yc, (k_out, v_out)


MEANS_BLKS = 4


def _cache_means_body(pt_ref, *refs):
    page_refs, o_ref = refs[:-1], refs[-1]
    for n in range(MEANS_BLKS):
        pages = page_refs[n * PAGES_PER_BLK:(n + 1) * PAGES_PER_BLK]
        tot = functools.reduce(lambda x, y: x + y, [jnp.sum(p[...], axis=0) for p in pages])
        o_ref[n] = tot / MOBA_BLOCK


def _cache_block_means(cache_k, pt_flat):
    def page(i):
        return pl.BlockSpec((None, None, PAGE_SIZE, C_HEADS, C_DIM),
                            lambda l, b, n, pt: (l, pt[b * N_PAGES + n * MEANS_BLKS * PAGES_PER_BLK + i], 0, 0, 0))
    n_pg = MEANS_BLKS * PAGES_PER_BLK
    return pl.pallas_call(
        _cache_means_body,
        out_shape=jax.ShapeDtypeStruct((DEPTH, DEC_BATCH, N_PAST_BLK, C_HEADS, C_DIM), F32),
        grid_spec=pltpu.PrefetchScalarGridSpec(
            num_scalar_prefetch=1,
            grid=(DEPTH, DEC_BATCH, N_PAST_BLK // MEANS_BLKS),
            in_specs=[page(i) for i in range(n_pg)],
            out_specs=pl.BlockSpec((None, None, MEANS_BLKS, C_HEADS, C_DIM),
                                   lambda l, b, n, pt: (l, b, n, 0, 0))),
        compiler_params=_cparams(("parallel", "parallel", "parallel")),
        name="cache_block_means",
    )(pt_flat, *([cache_k] * n_pg))


def _moba_gate_body(q_ref, means_ref, idx_ref):
    n_cand = DEC_BATCH * N_PAST_BLK
    q = q_ref[...]
    means = means_ref[...].reshape(n_cand, C_DIM)
    qh, ql = _split2(q)
    mh, ml = _split2(means)
    g = (lax.dot_general(qh, mh, _NT, preferred_element_type=F32)
         + lax.dot_general(qh, ml, _NT, preferred_element_type=F32)
         + lax.dot_general(ql, mh, _NT, preferred_element_type=F32))
    col = lax.broadcasted_iota(jnp.int32, (N_TOK_S, n_cand), 1)
    row = lax.broadcasted_iota(jnp.int32, (N_TOK_S, n_cand), 0)
    own_seq = (col >> int(math.log2(N_PAST_BLK))) == (row >> int(math.log2(DEC_SEQ)))
    g = jnp.where(own_seq, g, -jnp.inf)
    colf = col.astype(F32)
    lane = lax.broadcasted_iota(jnp.int32, (N_TOK_S, 128), 1)
    out = jnp.zeros((N_TOK_S, 128), jnp.int32)
    for r in range(MOBA_TOPK):
        mx = jnp.max(g, axis=1, keepdims=True)
        am = jnp.min(jnp.where(g == mx, colf, float(n_cand)), axis=1, keepdims=True)
        out = jnp.where(lane == r, am.astype(jnp.int32) & (N_PAST_BLK - 1), out)
        g = jnp.where(colf == am, -jnp.inf, g)
    idx_ref[...] = out


def _moba_gate(proj_s, means, layer):
    return pl.pallas_call(
        _moba_gate_body,
        out_shape=jax.ShapeDtypeStruct((C_HEADS, N_TOK_S, 128), jnp.int32),
        grid=(C_HEADS,),
        in_specs=[pl.BlockSpec((N_TOK_S, C_DIM), lambda h: (0, OFF_CQ + h)),
                  pl.BlockSpec((None, DEC_BATCH, N_PAST_BLK, C_DIM), lambda h: (layer, 0, 0, h))],
        out_specs=pl.BlockSpec((None, N_TOK_S, 128), lambda h: (h, 0, 0)),
        compiler_params=_cparams(("parallel",)),
        name="moba_gate",
    )(proj_s, means.reshape(DEPTH, DEC_BATCH, N_PAST_BLK, C_WIDTH))


def _moba_sample_body(idx_ref, pt_ref, rb_ref, q_ref, kn_ref, vn_ref, bs_ref, ck_ref, cv_ref, o_ref,
                      kbuf, vbuf, sem, *, layer):
    h = pl.program_id(0)
    b = pl.program_id(1)
    step = h * DEC_BATCH + b
    n_steps = C_HEADS * DEC_BATCH
    scale = C_DIM ** -0.5

    def block_of(hh, bb, s, r):
        return idx_ref[(hh * N_TOK_S + bb * DEC_SEQ + s) * MOBA_TOPK + r]

    def page_copies(st, slot):
        hh = st >> int(math.log2(DEC_BATCH))
        bb = st & (DEC_BATCH - 1)
        copies = []
        for s in range(DEC_SEQ):
            for r in range(MOBA_TOPK):
                n = block_of(hh, bb, s, r)
                for j in range(PAGES_PER_BLK):
                    pid = pt_ref[bb * N_PAGES + n * PAGES_PER_BLK + j]
                    i = (s * MOBA_TOPK + r) * PAGES_PER_BLK + j
                    copies.append(pltpu.make_async_copy(ck_ref.at[layer, pid, :, hh, :], kbuf.at[slot, i],
                                                        sem.at[0, slot]))
                    copies.append(pltpu.make_async_copy(cv_ref.at[layer, pid, :, hh, :], vbuf.at[slot, i],
                                                        sem.at[1, slot]))
        return copies

    @pl.when(step == 0)
    def _():
        for c in page_copies(step, 0):
            c.start()

    @pl.when(step + 1 < n_steps)
    def _():
        for c in page_copies(step + 1, (step + 1) & 1):
            c.start()

    slot = step & 1
    for c in page_copies(step, slot):
        c.wait()

    far_bias = rb_ref[REL_BUCKETS - 1, h]
    kn = _pad_frame(kn_ref[...]).astype(BF16)
    vn = _pad_frame(vn_ref[...]).astype(BF16)
    lane = lax.broadcasted_iota(jnp.int32, (1, FRAME), 1)
    for s in range(DEC_SEQ):
        row = b * DEC_SEQ + s
        q8 = jnp.broadcast_to(q_ref[s:s + 1, :], (8, C_DIM)).astype(BF16)
        logits = []
        for r in range(MOBA_TOPK):
            newest = block_of(h, b, s, r) == N_PAST_BLK - 1
            for j in range(PAGES_PER_BLK):
                kp = kbuf[slot, (s * MOBA_TOPK + r) * PAGES_PER_BLK + j].astype(BF16)
                lg = lax.dot_general(q8, kp, _NT, preferred_element_type=F32)[0:1] * scale
                bias = jnp.where(newest, bs_ref[s:s + 1, j * PAGE_SIZE:(j + 1) * PAGE_SIZE], far_bias)
                logits.append(lg + bias)
        lg = lax.dot_general(q8, kn, _NT, preferred_element_type=F32)[0:1] * scale
        ob = jnp.zeros((1, FRAME), F32)
        for d in range(DEC_SEQ):
            ob = jnp.where(lane == row - d, rb_ref[d, h], ob)
        valid = (lane >= b * DEC_SEQ) & (lane <= row)
        logits.append(jnp.where(valid, lg + ob, NEG_BIG))

        mx = jnp.max(functools.reduce(jnp.maximum, logits), axis=1, keepdims=True)
        ps = [jnp.exp(l - mx) for l in logits]
        denom = functools.reduce(lambda x, y: x + y, [jnp.sum(p, axis=1, keepdims=True) for p in ps])
        vals = [vbuf[slot, s * MOBA_TOPK * PAGES_PER_BLK + i].astype(BF16)
                for i in range(MOBA_TOPK * PAGES_PER_BLK)] + [vn]
        out = jnp.zeros((8, C_DIM), F32)
        for p, vv in zip(ps, vals):
            p8 = jnp.broadcast_to(p, (8, p.shape[1])).astype(BF16)
            out = out + jnp.dot(p8, vv, preferred_element_type=F32)
        o_ref[s:s + 1, :] = out[0:1] / denom


def _moba_sample(proj_s, cache_k, cache_v, idx_flat, pt_flat, rel_bias, bias_s, layer):
    n_pg = DEC_SEQ * MOBA_TOPK * PAGES_PER_BLK
    proj3 = proj_s.reshape(DEC_BATCH, DEC_SEQ, N_MAIN)
    out = pl.pallas_call(
        functools.partial(_moba_sample_body, layer=layer),
        out_shape=jax.ShapeDtypeStruct((DEC_BATCH, DEC_SEQ, C_WIDTH), F32),
        grid_spec=pltpu.PrefetchScalarGridSpec(
            num_scalar_prefetch=2,
            grid=(C_HEADS, DEC_BATCH),
            in_specs=[pl.BlockSpec(memory_space=pltpu.SMEM),
                      pl.BlockSpec((None, DEC_SEQ, C_DIM), lambda h, b, idx, pt: (b, 0, OFF_CQ + h)),
                      pl.BlockSpec((N_TOK_S, C_DIM), lambda h, b, idx, pt: (0, OFF_CK + h)),
                      pl.BlockSpec((N_TOK_S, C_DIM), lambda h, b, idx, pt: (0, OFF_CV + h)),
                      pl.BlockSpec((None, 8, MOBA_BLOCK), lambda h, b, idx, pt: (h, 0, 0)),
                      pl.BlockSpec(memory_space=pl.ANY),
                      pl.BlockSpec(memory_space=pl.ANY)],
            out_specs=pl.BlockSpec((None, DEC_SEQ, C_DIM), lambda h, b, idx, pt: (b, 0, h)),
            scratch_shapes=[pltpu.VMEM((2, n_pg, PAGE_SIZE, C_DIM), F32),
                            pltpu.VMEM((2, n_pg, PAGE_SIZE, C_DIM), F32),
                            pltpu.SemaphoreType.DMA((2, 2))]),
        compiler_params=_cparams(("arbitrary", "arbitrary")),
        name="moba_sample",
    )(idx_flat, pt_flat, rel_bias, proj3, proj_s, proj_s, bias_s, cache_k, cache_v)
    return out.reshape(N_TOK_S, C_WIDTH)


def _merge_body(ya_ref, yb_ref, yc_ref, wa_ref, wb_ref, wc_ref, g0_ref, g1_ref, g2_ref, o_ref):
    acc = g0_ref[...] * jnp.dot(ya_ref[...], wa_ref[...], preferred_element_type=F32)
    acc = acc + g1_ref[...] * jnp.dot(yb_ref[...], wb_ref[...], preferred_element_type=F32)
    acc = acc + g2_ref[...] * jnp.dot(yc_ref[...], wc_ref[...], preferred_element_type=F32)
    o_ref[...] = acc.astype(o_ref.dtype)


def _merge(ya, yb, yc, gates, w_a, w_b, w_c, layer, *, tm, tn):
    m = ya.shape[0]
    nb = D_MODEL // tn
    act = lambda width: pl.BlockSpec((tm, width), lambda j, i: (i, 0))
    wgt = lambda width: pl.BlockSpec((None, width, tn), lambda j, i: (layer, 0, j))
    gate = lambda br: pl.BlockSpec((tm, tn), lambda j, i: (i, br * nb + j))
    return pl.pallas_call(
        _merge_body,
        out_shape=jax.ShapeDtypeStruct((m, D_MODEL), BF16),
        grid=(nb, m // tm),
        in_specs=[act(A_WIDTH), act(B_WIDTH), act(C_WIDTH), wgt(A_WIDTH), wgt(B_WIDTH), wgt(C_WIDTH),
                  gate(0), gate(1), gate(2)],
        out_specs=pl.BlockSpec((tm, tn), lambda j, i: (i, j)),
        compiler_params=_cparams(("parallel", "parallel")),
        name="branch_merge",
    )(ya, yb, yc, w_a, w_b, w_c, gates, gates, gates)


def kernel(x_prompt, x_sample, cache_k, cache_v, state_hgrn, state_conv, page_table, w_in, b_gate,
           conv_w, hgrn_lb_logits, hgrn_norm_g, w_a_up, w_b_up, w_c_up, w_o, rel_bias, g_pre_mix,
           g_post_mix, g_pre_mlp, g_post_mlp, w_mlp_up, w_mlp_down):
    consts = {k: jnp.asarray(v) for k, v in _constants().items()}
    for name in ("m3_p", "emask_p", "m3_s", "emask_s"):
        consts[name] = consts[name].astype(BF16)

    w_in_b, w_a_b, w_b_b, w_c_b = (w.astype(BF16) for w in (w_in, w_a_up, w_b_up, w_c_up))
    w_o_b, w_up_b, w_dn_b = (w.astype(BF16) for w in (w_o, w_mlp_up, w_mlp_down))
    vec = lambda g: g.reshape(DEPTH, 1, g.shape[-1])
    b_gate3, ng3 = vec(b_gate), vec(hgrn_norm_g)
    gpm, gqm, gpl, gql = vec(g_pre_mix), vec(g_post_mix), vec(g_pre_mlp), vec(g_post_mlp)
    pt_flat = page_table.reshape(-1)

    lbp = _lower_bound_params(hgrn_lb_logits)
    bown, bprev, bias_s = _bias_tables(rel_bias, consts)
    means = _cache_block_means(cache_k, pt_flat)

    mm = dict(tm=1024, tn=1024, tk=D_MODEL)
    mm_down = dict(tm=2048, tn=1024, tk=2048)
    tm_p, tm_s = 256, N_TOK_S
    xp = x_prompt.reshape(N_TOK_P, D_MODEL)
    xs = x_sample.reshape(N_TOK_S, D_MODEL)
    hp = _prenorm(xp, gpm, 0, tm=tm_p)
    hs = _prenorm(xs, gpm, 0, tm=tm_s)
    outs = {k: [] for k in ("ks", "vs", "sp", "ss", "cp", "cs")}
    kv_shape_p = (DEPTH, BATCH, SEQ, C_HEADS, C_DIM)
    kv_p = (jnp.zeros(kv_shape_p, F32), jnp.zeros(kv_shape_p, F32))
    kv_shape_s = (DEC_BATCH, DEC_SEQ, C_HEADS, C_DIM)

    for l in range(DEPTH):
        proj_p, proj_s = _matmul(hp, hs, w_in_b, l, col0=0, n_cols=N_MAIN, name="in_proj", **mm)
        gates_p, gates_s = _matmul(hp, hs, w_in_b, l, col0=N_MAIN, n_cols=N_GATE, epilogue="sigmoid_bias",
                                   bias=b_gate3, name="in_proj_gates", **mm)
        ya_p, st_p = _hgrn_prompt(proj_p, lbp, ng3, l, consts)
        yb_p, cn_p = _shortconv(proj_p.reshape(BATCH, SEQ, N_MAIN), conv_w, l, None, tc=256, out_dtype=BF16)
        yc_p, kv_p = _moba_prompt(proj_p, kv_p, l, rel_bias, bown, bprev)
        ya_s, st_s = _hgrn_sample(proj_s, state_hgrn, lbp, ng3, l, consts)
        yb_s, cn_s = _shortconv(proj_s.reshape(DEC_BATCH, DEC_SEQ, N_MAIN), conv_w, l, state_conv,
                                tc=B_WIDTH, out_dtype=F32)
        idx = _moba_gate(proj_s, means, l)
        idx_flat = idx[:, :, :MOBA_TOPK].reshape(-1)
        yc_s = _moba_sample(proj_s, cache_k, cache_v, idx_flat, pt_flat, rel_bias, bias_s, l)
        outs["ks"].append(proj_s[:, OFF_CK * 128:OFF_CK * 128 + C_WIDTH].reshape(kv_shape_s))
        outs["vs"].append(proj_s[:, OFF_CV * 128:OFF_CV * 128 + C_WIDTH].reshape(kv_shape_s))
        outs["sp"].append(st_p)
        outs["ss"].append(st_s)
        outs["cp"].append(cn_p)
        outs["cs"].append(cn_s)

        merged_p = _merge(ya_p, yb_p.reshape(N_TOK_P, B_WIDTH), yc_p, gates_p, w_a_b, w_b_b, w_c_b, l,
                          tm=512, tn=1024)
        merged_s = _merge(ya_s, yb_s.reshape(N_TOK_S, B_WIDTH).astype(BF16), yc_s.astype(BF16), gates_s,
                          w_a_b, w_b_b, w_c_b, l, tm=N_TOK_S, tn=1024)
        mix_p, mix_s = _matmul(merged_p, merged_s, w_o_b, l, name="out_proj", **mm)
        xp, hm_p = _post(xp, mix_p, gqm, l, gpl, l, tm=tm_p)
        xs, hm_s = _post(xs, mix_s, gqm, l, gpl, l, tm=tm_s)
        up_p, up_s = _matmul(hm_p, hm_s, w_up_b, l, epilogue="relu2", out_dtype=BF16, name="mlp_up", **mm)
        dn_p, dn_s = _matmul(up_p, up_s, w_dn_b, l, name="mlp_down", **mm_down)
        nxt = (gpm, l + 1) if l + 1 < DEPTH else ()
        xp, hp = _post(xp, dn_p, gql, l, *nxt, tm=tm_p)
        xs, hs = _post(xs, dn_s, gql, l, *nxt, tm=tm_s)

    return (xp.reshape(BATCH, SEQ, D_MODEL), xs.reshape(DEC_BATCH, DEC_SEQ, D_MODEL),
            kv_p[0], kv_p[1], jnp.stack(outs["ks"]), jnp.stack(outs["vs"]),
            jnp.stack(outs["sp"]), jnp.stack(outs["ss"]), jnp.stack(outs["cp"]), jnp.stack(outs["cs"]))
```

```python
import functools
import math

import numpy as np
import jax
import jax.numpy as jnp
from jax import lax
from jax.experimental import pallas as pl
from jax.experimental.pallas import tpu as pltpu

F32 = jnp.float32
BF16 = jnp.bfloat16

D_MODEL = 4096
BATCH = 4
SEQ = 2048
DEPTH = 4
DEC_BATCH = 8
DEC_SEQ = 4
PAST_LEN = 8192
PAGE_SIZE = 128
A_HEADS = 8
A_DIM = 128
A_WIDTH = A_HEADS * A_DIM
A_CHUNK = 16
B_WIDTH = D_MODEL // 4
C_HEADS = 16
C_DIM = 128
C_WIDTH = C_HEADS * C_DIM
MOBA_BLOCK = 256
MOBA_TOPK = 3
REL_BUCKETS = 32
REL_MAX_DIST = 128
D_FF = 4 * D_MODEL
NORM_EPS = 1e-6
NEG_BIG = -1e30
LB_FLOOR = 1e-30

N_TOK_P = BATCH * SEQ
N_TOK_S = DEC_BATCH * DEC_SEQ
N_MAIN = 4 * A_WIDTH + 3 * B_WIDTH + 3 * C_WIDTH
N_GATE = 3 * D_MODEL
OFF_AQ, OFF_AF, OFF_AI, OFF_AG = 0, 8, 16, 24
OFF_BB, OFF_BC, OFF_BX = 32, 40, 48
OFF_CQ, OFF_CK, OFF_CV = 56, 72, 88
N_PAGES = PAST_LEN // PAGE_SIZE
N_PAST_BLK = PAST_LEN // MOBA_BLOCK
PAGES_PER_BLK = MOBA_BLOCK // PAGE_SIZE
FRAME = 128
HGRN_UNROLL = 8

V7X_VMEM_LIMIT = 48 * 1024 * 1024
V7X_VMEM_BIG = 56 * 1024 * 1024

_NT = (((1,), (1,)), ((), ()))


def _cparams(sem, vmem=V7X_VMEM_LIMIT):
    return pltpu.CompilerParams(dimension_semantics=sem, vmem_limit_bytes=vmem)


def _sigmoid(x):
    return 1.0 / (1.0 + jnp.exp(-x))


def _split3(x):
    hi = x.astype(BF16)
    r1 = x - hi.astype(F32)
    lo = r1.astype(BF16)
    lo2 = (r1 - lo.astype(F32)).astype(BF16)
    return hi, lo, lo2


def _split2(x):
    hi = x.astype(BF16)
    lo = (x - hi.astype(F32)).astype(BF16)
    return hi, lo


def _rel_bucket_np(rel):
    n = np.maximum(rel, 0)
    max_exact = REL_BUCKETS // 2
    nf = np.maximum(n, max_exact).astype(np.float32)
    large = max_exact + (np.log(nf / max_exact) / math.log(REL_MAX_DIST / max_exact)
                         * (REL_BUCKETS - max_exact)).astype(np.int32)
    large = np.clip(large, max_exact, REL_BUCKETS - 1)
    return np.where(n < max_exact, n, large).astype(np.int32)


def _chunk_mats(n, chunk):
    r = np.arange(n)[:, None]
    c = np.arange(n)[None, :]
    same = (r // chunk) == (c // chunk)
    t_cum = same & (c <= r)
    t_last = same
    t_ref = same & ((c % chunk) <= chunk // 2)
    m3 = np.concatenate([t_cum, t_last, t_ref], axis=0).astype(np.float32)
    return m3, t_cum.astype(np.float32)


@functools.lru_cache(maxsize=None)
def _constants():
    m3_p, causal_p = _chunk_mats(MOBA_BLOCK, A_CHUNK)
    rows = np.arange(16 * A_DIM)[:, None]
    cols = np.arange(256)[None, :]
    emask_p = ((rows // A_DIM) == (cols // A_CHUNK)).astype(np.float32)
    m3_s, causal_s = _chunk_mats(FRAME, DEC_SEQ)
    rows = np.arange(DEC_BATCH * A_DIM)[:, None]
    cols = np.arange(FRAME)[None, :]
    emask_s = (((rows // A_DIM) == (cols // DEC_SEQ)) & (cols < N_TOK_S)).astype(np.float32)
    kl = np.arange(MOBA_BLOCK)[:, None]
    ql = np.arange(MOBA_BLOCK)[None, :]
    bucket_own = _rel_bucket_np(ql - kl)
    bucket_prev = _rel_bucket_np(MOBA_BLOCK + ql - kl)
    s = np.arange(8)[:, None]
    t = np.arange(MOBA_BLOCK)[None, :]
    bucket_s = _rel_bucket_np(MOBA_BLOCK + s - t)
    return dict(m3_p=m3_p, causal_p=causal_p, emask_p=emask_p, m3_s=m3_s, causal_s=causal_s,
                emask_s=emask_s, bucket_own=bucket_own, bucket_prev=bucket_prev, bucket_s=bucket_s)


def _mm_body(*refs, nk, epilogue, tm):
    refs = list(refs)
    a_ref, w_ref = refs[:2]
    b_ref = refs[2] if epilogue == "sigmoid_bias" else None
    as_ref, o_ref, os_ref, acat = refs[-4:]
    ms = as_ref.shape[0]
    i = pl.program_id(0)

    def epi(r):
        if epilogue == "sigmoid_bias":
            r = _sigmoid(r + b_ref[...])
        elif epilogue == "relu2":
            r = jnp.square(jnp.maximum(r, 0.0))
        return r

    def both_groups():
        acat[0:tm, :] = a_ref[...]
        acat[tm:tm + ms, :] = as_ref[...]
        return jnp.dot(acat[...], w_ref[...], preferred_element_type=F32)

    if nk == 1:
        @pl.when(i == 0)
        def _():
            r = both_groups()
            o_ref[...] = epi(r[0:tm]).astype(o_ref.dtype)
            os_ref[...] = epi(r[tm:tm + ms]).astype(os_ref.dtype)

        @pl.when(i > 0)
        def _():
            o_ref[...] = epi(jnp.dot(a_ref[...], w_ref[...], preferred_element_type=F32)).astype(o_ref.dtype)
        return

    k = pl.program_id(2)

    @pl.when(k == 0)
    def _():
        o_ref[...] = jnp.zeros(o_ref.shape, F32)

    @pl.when((k == 0) & (i == 0))
    def _():
        os_ref[...] = jnp.zeros(os_ref.shape, F32)

    @pl.when(i == 0)
    def _():
        r = both_groups()
        o_ref[...] += r[0:tm]
        os_ref[...] += r[tm:tm + ms]

    @pl.when(i > 0)
    def _():
        o_ref[...] += jnp.dot(a_ref[...], w_ref[...], preferred_element_type=F32)


def _matmul(a, a_s, w, layer, *, col0=0, n_cols=None, tm, tn, tk, epilogue="none", bias=None,
            out_dtype=F32, name):
    m, kdim = a.shape
    ms = a_s.shape[0]
    n_cols = w.shape[2] if n_cols is None else n_cols
    assert m % tm == 0 and kdim % tk == 0 and n_cols % tn == 0 and col0 % tn == 0
    nk = kdim // tk
    nj = n_cols // tn
    assert nk == 1 or (epilogue == "none" and out_dtype == F32)
    cb = col0 // tn
    in_specs = [pl.BlockSpec((tm, tk), lambda i, j, k: (i, k)),
                pl.BlockSpec((None, tk, tn), lambda i, j, k: (layer, k, j + cb))]
    args = [a, w]
    if epilogue == "sigmoid_bias":
        in_specs.append(pl.BlockSpec((None, 1, tn), lambda i, j, k: (layer, 0, j)))
        args.append(bias)
    in_specs.append(pl.BlockSpec((ms, tk), lambda i, j, k: (0, k)))
    args.append(a_s)
    out_specs = [pl.BlockSpec((tm, tn), lambda i, j, k: (i, j)),
                 pl.BlockSpec((ms, tn), lambda i, j, k: (0, jnp.where(i == 0, j, nj - 1)))]
    return pl.pallas_call(
        functools.partial(_mm_body, nk=nk, epilogue=epilogue, tm=tm),
        out_shape=[jax.ShapeDtypeStruct((m, n_cols), out_dtype), jax.ShapeDtypeStruct((ms, n_cols), out_dtype)],
        grid=(m // tm, nj, nk),
        in_specs=in_specs,
        out_specs=out_specs,
        scratch_shapes=[pltpu.VMEM((tm + ms, tk), BF16)],
        compiler_params=_cparams(("arbitrary", "arbitrary", "arbitrary"), V7X_VMEM_BIG),
        name=name,
    )(*args)


def _rms(x, g):
    return x * lax.rsqrt(jnp.mean(x * x, axis=-1, keepdims=True) + NORM_EPS) * g


def _prenorm_body(x_ref, g_ref, h_ref):
    h_ref[...] = _rms(x_ref[...], g_ref[...]).astype(h_ref.dtype)


def _prenorm(x, g, layer, *, tm):
    m = x.shape[0]
    return pl.pallas_call(
        _prenorm_body,
        out_shape=jax.ShapeDtypeStruct((m, D_MODEL), BF16),
        grid=(m // tm,),
        in_specs=[pl.BlockSpec((tm, D_MODEL), lambda i: (i, 0)),
                  pl.BlockSpec((None, 1, D_MODEL), lambda i: (layer, 0, 0))],
        out_specs=pl.BlockSpec((tm, D_MODEL), lambda i: (i, 0)),
        compiler_params=_cparams(("parallel",)),
        name="prenorm",
    )(x, g)


def _post_body(x_ref, y_ref, gpost_ref, *rest, with_next):
    xn = x_ref[...] + _rms(y_ref[...], gpost_ref[...])
    if with_next:
        gnext_ref, xo_ref, ho_ref = rest
        ho_ref[...] = _rms(xn, gnext_ref[...]).astype(ho_ref.dtype)
    else:
        (xo_ref,) = rest
    xo_ref[...] = xn


def _post(x, y, gpost, layer, gnext=None, next_layer=None, *, tm):
    m = x.shape[0]
    with_next = gnext is not None
    row = pl.BlockSpec((tm, D_MODEL), lambda i: (i, 0))
    in_specs = [row, row, pl.BlockSpec((None, 1, D_MODEL), lambda i: (layer, 0, 0))]
    args = [x, y, gpost]
    out_shape = [jax.ShapeDtypeStruct((m, D_MODEL), F32)]
    out_specs = [row]
    if with_next:
        in_specs.append(pl.BlockSpec((None, 1, D_MODEL), lambda i: (next_layer, 0, 0)))
        args.append(gnext)
        out_shape.append(jax.ShapeDtypeStruct((m, D_MODEL), BF16))
        out_specs.append(row)
    res = pl.pallas_call(
        functools.partial(_post_body, with_next=with_next),
        out_shape=out_shape,
        grid=(m // tm,),
        in_specs=in_specs,
        out_specs=out_specs,
        compiler_params=_cparams(("parallel",)),
        name="post_norm",
    )(*args)
    return (res[0], res[1]) if with_next else (res[0], None)


def _lb_body(x_ref, o_ref):
    x = x_ref[...]
    rows = [x[l:l + 1] for l in range(DEPTH)]
    mx = functools.reduce(jnp.maximum, rows)
    es = [jnp.exp(r - mx) for r in rows]
    tot = functools.reduce(lambda a, b: a + b, es)
    ps = [e / tot for e in es]
    cum = None
    for l in range(DEPTH):
        cum = ps[l] if cum is None else cum + ps[l]
        lb = jnp.clip(cum - ps[0], 0.0, 1.0)
        o_ref[3 * l:3 * l + 1, :] = jnp.log(jnp.maximum(lb, LB_FLOOR))
        o_ref[3 * l + 1:3 * l + 2, :] = jnp.log1p(-lb)
        o_ref[3 * l + 2:3 * l + 3, :] = 1.0 - lb


def _lower_bound_params(lb_logits):
    out = pl.pallas_call(
        _lb_body,
        out_shape=jax.ShapeDtypeStruct((3 * DEPTH, A_WIDTH), F32),
        name="hgrn_lower_bounds",
    )(lb_logits)
    return out.reshape(DEPTH, 3, A_HEADS, 1, A_DIM)


def _hgrn_gates(aq, z, la, l1m, oml):
    q = aq * _sigmoid(aq)
    log_sig = jnp.minimum(z, 0.0) - jnp.log1p(jnp.exp(-jnp.abs(z)))
    b = l1m + log_sig
    log_f = jnp.maximum(la, b) + jnp.log1p(jnp.exp(-jnp.abs(la - b)))
    k = oml / (1.0 + jnp.exp(z))
    return q, k, log_f


def _chunk_sums(m3, log_f, n):
    hi, lo, lo2 = _split3(log_f)
    c3 = (jnp.dot(m3, hi, preferred_element_type=F32)
          + jnp.dot(m3, lo, preferred_element_type=F32)
          + jnp.dot(m3, lo2, preferred_element_type=F32))
    return c3[0:n], c3[n:2 * n], c3[2 * n:3 * n]


def _hgrn_prompt_body(aq_ref, af_ref, ai_ref, ag_ref, lbp_ref, ng_ref, m3_ref, cmask_ref, emask_ref,
                      ya_ref, st_ref, o_scr, qc_scr, dec_scr, ut_scr):
    blk = MOBA_BLOCK
    n_blk = SEQ // blk
    cpb = blk // A_CHUNK
    la = lbp_ref[0]
    l1m = lbp_ref[1]
    oml = lbp_ref[2]

    def block_body(j, carry):
        rows = pl.ds(pl.multiple_of(j * blk, blk), blk)
        q, k, log_f = _hgrn_gates(aq_ref[rows, :], af_ref[rows, :], la, l1m, oml)
        v = ai_ref[rows, :]
        cum, last_b, ref_b = _chunk_sums(m3_ref[...], log_f, blk)
        q_rel = (q * jnp.exp(cum - ref_b)).astype(BF16)
        k_rel = (k * jnp.exp(ref_b - cum)).astype(BF16)
        k_last = (k * jnp.exp(last_b - cum)).astype(BF16)
        a = lax.dot_general(q_rel, k_rel, _NT, preferred_element_type=F32)
        a = jnp.where(cmask_ref[...] > 0.0, a, 0.0)
        o_scr[rows, :] = jnp.dot(a.astype(BF16), v.astype(BF16), preferred_element_type=F32)
        qc_scr[rows, :] = (q * jnp.exp(cum)).astype(BF16)
        dec_scr[rows, :] = jnp.exp(last_b)
        vt = v.T.astype(BF16)
        vt_exp = jnp.concatenate([vt] * cpb, axis=0) * emask_ref[...]
        ut_rows = pl.ds(pl.multiple_of(j * (cpb * A_DIM), cpb * A_DIM), cpb * A_DIM)
        ut_scr[ut_rows, :] = jnp.dot(vt_exp, k_last, preferred_element_type=F32)
        return carry

    lax.fori_loop(0, n_blk, block_body, 0, unroll=HGRN_UNROLL)

    def scan_body(n, st):
        rows = pl.ds(pl.multiple_of(n * A_DIM, A_DIM), A_DIM)
        u = ut_scr[rows, :]
        ut_scr[rows, :] = st
        d = dec_scr[pl.ds(pl.multiple_of(n * A_CHUNK, A_CHUNK), 8), :][0:1, :]
        return st * d + u

    st = lax.fori_loop(0, SEQ // A_CHUNK, scan_body, jnp.zeros((A_DIM, A_DIM), F32), unroll=4)
    st_ref[...] = st.T

    ng = ng_ref[...]

    def out_body(j, carry):
        rows = pl.ds(pl.multiple_of(j * blk, blk), blk)
        ut_rows = pl.ds(pl.multiple_of(j * (cpb * A_DIM), cpb * A_DIM), cpb * A_DIM)
        starts = ut_scr[ut_rows, :].astype(BF16)
        r = lax.dot_general(qc_scr[rows, :], starts, _NT, preferred_element_type=F32)
        o_inter = jnp.concatenate(
            [r[n * A_CHUNK:(n + 1) * A_CHUNK, n * A_DIM:(n + 1) * A_DIM] for n in range(cpb)], axis=0)
        o = _rms(o_scr[rows, :] + o_inter, ng)
        ag = ag_ref[rows, :]
        ya_ref[rows, :] = (o * (ag * _sigmoid(ag))).astype(ya_ref.dtype)
        return carry

    lax.fori_loop(0, n_blk, out_body, 0, unroll=HGRN_UNROLL)


def _hgrn_prompt(proj, lbp, norm_g, layer, consts):
    col = lambda off: pl.BlockSpec((SEQ, A_DIM), lambda b, h: (b, off + h))
    whole = lambda shape: pl.BlockSpec(shape, lambda b, h: (0,) * len(shape))
    return pl.pallas_call(
        _hgrn_prompt_body,
        out_shape=[jax.ShapeDtypeStruct((N_TOK_P, A_WIDTH), BF16),
                   jax.ShapeDtypeStruct((BATCH, A_HEADS, A_DIM, A_DIM), F32)],
        grid=(BATCH, A_HEADS),
        in_specs=[col(OFF_AQ), col(OFF_AF), col(OFF_AI), col(OFF_AG),
                  pl.BlockSpec((None, 3, None, 1, A_DIM), lambda b, h: (layer, 0, h, 0, 0)),
                  pl.BlockSpec((None, 1, A_DIM), lambda b, h: (layer, 0, 0)),
                  whole((3 * MOBA_BLOCK, MOBA_BLOCK)), whole((MOBA_BLOCK, MOBA_BLOCK)),
                  whole((16 * A_DIM, MOBA_BLOCK))],
        out_specs=[pl.BlockSpec((SEQ, A_DIM), lambda b, h: (b, h)),
                   pl.BlockSpec((None, None, A_DIM, A_DIM), lambda b, h: (b, h, 0, 0))],
        scratch_shapes=[pltpu.VMEM((SEQ, A_DIM), F32), pltpu.VMEM((SEQ, A_DIM), BF16),
                        pltpu.VMEM((SEQ, A_DIM), F32), pltpu.VMEM((SEQ // A_CHUNK * A_DIM, A_DIM), F32)],
        compiler_params=_cparams(("parallel", "parallel")),
        name="hgrn_prompt",
    )(proj, proj, proj, proj, lbp, norm_g, consts["m3_p"], consts["causal_p"], consts["emask_p"])


def _pad_frame(x):
    return jnp.concatenate([x, jnp.zeros((FRAME - x.shape[0], x.shape[1]), x.dtype)], axis=0)


def _hgrn_sample_body(aq_ref, af_ref, ai_ref, ag_ref, s0_ref, lbp_ref, ng_ref, m3_ref, cmask_ref,
                      emask_ref, ya_ref, sn_ref):
    la = lbp_ref[0]
    l1m = lbp_ref[1]
    oml = lbp_ref[2]
    q, k, log_f = _hgrn_gates(_pad_frame(aq_ref[...]), _pad_frame(af_ref[...]), la, l1m, oml)
    v = _pad_frame(ai_ref[...])
    cum, last_b, ref_b = _chunk_sums(m3_ref[...], log_f, FRAME)
    q_rel = (q * jnp.exp(cum - ref_b)).astype(BF16)
    k_rel = (k * jnp.exp(ref_b - cum)).astype(BF16)
    k_last = (k * jnp.exp(last_b - cum)).astype(BF16)
    a = lax.dot_general(q_rel, k_rel, _NT, preferred_element_type=F32)
    a = jnp.where(cmask_ref[...] > 0.0, a, 0.0)
    o_intra = jnp.dot(a.astype(BF16), v.astype(BF16), preferred_element_type=F32)
    qc = (q * jnp.exp(cum)).astype(BF16)
    dec = jnp.exp(last_b)
    vt = v.T.astype(BF16)
    vt_exp = jnp.concatenate([vt] * DEC_BATCH, axis=0) * emask_ref[...]
    ut = jnp.dot(vt_exp, k_last, preferred_element_type=F32)
    s0t = jnp.concatenate([s0_ref[b].T for b in range(DEC_BATCH)], axis=0)
    r = lax.dot_general(qc, s0t.astype(BF16), _NT, preferred_element_type=F32)
    row_b = lax.broadcasted_iota(jnp.int32, (FRAME, A_DIM), 0) >> int(math.log2(DEC_SEQ))
    o_inter = jnp.zeros((FRAME, A_DIM), F32)
    for b in range(DEC_BATCH):
        o_inter = jnp.where(row_b == b, r[:, b * A_DIM:(b + 1) * A_DIM], o_inter)
        d = dec[b * DEC_SEQ:b * DEC_SEQ + 1, :]
        snt = s0t[b * A_DIM:(b + 1) * A_DIM, :] * d + ut[b * A_DIM:(b + 1) * A_DIM, :]
        sn_ref[b] = snt.T
    o = _rms(o_intra + o_inter, ng_ref[...])
    ag = _pad_frame(ag_ref[...])
    ya_ref[...] = (o * (ag * _sigmoid(ag)))[0:N_TOK_S].astype(ya_ref.dtype)


def _hgrn_sample(proj, state, lbp, norm_g, layer, consts):
    col = lambda off: pl.BlockSpec((N_TOK_S, A_DIM), lambda h: (0, off + h))
    whole = lambda shape: pl.BlockSpec(shape, lambda h: (0,) * len(shape))
    return pl.pallas_call(
        _hgrn_sample_body,
        out_shape=[jax.ShapeDtypeStruct((N_TOK_S, A_WIDTH), BF16),
                   jax.ShapeDtypeStruct((DEC_BATCH, A_HEADS, A_DIM, A_DIM), F32)],
        grid=(A_HEADS,),
        in_specs=[col(OFF_AQ), col(OFF_AF), col(OFF_AI), col(OFF_AG),
                  pl.BlockSpec((None, DEC_BATCH, None, A_DIM, A_DIM), lambda h: (layer, 0, h, 0, 0)),
                  pl.BlockSpec((None, 3, None, 1, A_DIM), lambda h: (layer, 0, h, 0, 0)),
                  pl.BlockSpec((None, 1, A_DIM), lambda h: (layer, 0, 0)),
                  whole((3 * FRAME, FRAME)), whole((FRAME, FRAME)), whole((DEC_BATCH * A_DIM, FRAME))],
        out_specs=[pl.BlockSpec((N_TOK_S, A_DIM), lambda h: (0, h)),
                   pl.BlockSpec((DEC_BATCH, None, A_DIM, A_DIM), lambda h: (0, h, 0, 0))],
        compiler_params=_cparams(("parallel",)),
        name="hgrn_sample",
    )(proj, proj, proj, proj, state, lbp, norm_g, consts["m3_s"], consts["causal_s"], consts["emask_s"])


def _conv_body(bb_ref, bc_ref, bx_ref, w_ref, *rest, t, has_prev):
    if has_prev:
        prev_ref, y_ref, cn_ref, scr = rest
    else:
        y_ref, cn_ref, scr = rest
    u = bc_ref[...] * bx_ref[...]
    scr[0:8, :] = jnp.zeros((8, scr.shape[1]), F32)
    if has_prev:
        scr[6:8, :] = prev_ref[...]
    scr[8:8 + t, :] = u
    w = w_ref[...]
    y = w[0:1] * scr[6:6 + t, :] + w[1:2] * scr[7:7 + t, :] + w[2:3] * u
    y_ref[...] = (bb_ref[...] * y).astype(y_ref.dtype)
    cn_ref[...] = scr[6 + t:8 + t, :]


def _shortconv(proj3, conv_w, layer, prev, *, tc, out_dtype):
    bsz, t, _ = proj3.shape
    has_prev = prev is not None
    lanes = lambda off: pl.BlockSpec((None, t, tc), lambda b, c: (b, 0, off * 128 // tc + c))
    in_specs = [lanes(OFF_BB), lanes(OFF_BC), lanes(OFF_BX),
                pl.BlockSpec((None, 3, tc), lambda b, c: (layer, 0, c))]
    args = [proj3, proj3, proj3, conv_w]
    if has_prev:
        in_specs.append(pl.BlockSpec((None, None, 2, tc), lambda b, c: (layer, b, 0, c)))
        args.append(prev)
    return pl.pallas_call(
        functools.partial(_conv_body, t=t, has_prev=has_prev),
        out_shape=[jax.ShapeDtypeStruct((bsz, t, B_WIDTH), out_dtype),
                   jax.ShapeDtypeStruct((bsz, 2, B_WIDTH), F32)],
        grid=(bsz, B_WIDTH // tc),
        in_specs=in_specs,
        out_specs=[pl.BlockSpec((None, t, tc), lambda b, c: (b, 0, c)),
                   pl.BlockSpec((None, 2, tc), lambda b, c: (b, 0, c))],
        scratch_shapes=[pltpu.VMEM((t + 8, tc), F32)],
        compiler_params=_cparams(("parallel", "parallel")),
        name="shortconv",
    )(*args)


def _bias_body(rb_ref, bo_ref, bp_ref, bs_ref, own_ref, prev_ref, smp_ref):
    h = pl.program_id(0)

    def lookup(bucket):
        acc = jnp.zeros(bucket.shape, F32)
        for b in range(REL_BUCKETS):
            acc = jnp.where(bucket == b, rb_ref[b, h], acc)
        return acc

    blk = MOBA_BLOCK
    key_i = lax.broadcasted_iota(jnp.int32, (blk, blk), 0)
    qry_i = lax.broadcasted_iota(jnp.int32, (blk, blk), 1)
    own_ref[...] = jnp.where(key_i <= qry_i, lookup(bo_ref[...]), NEG_BIG)
    prev_ref[...] = lookup(bp_ref[...])
    smp_ref[...] = lookup(bs_ref[...])


def _bias_tables(rel_bias, consts):
    whole = lambda shape: pl.BlockSpec(shape, lambda h: (0,) * len(shape))
    blk = MOBA_BLOCK
    return pl.pallas_call(
        _bias_body,
        out_shape=[jax.ShapeDtypeStruct((C_HEADS, blk, blk), F32),
                   jax.ShapeDtypeStruct((C_HEADS, blk, blk), F32),
                   jax.ShapeDtypeStruct((C_HEADS, 8, blk), F32)],
        grid=(C_HEADS,),
        in_specs=[pl.BlockSpec(memory_space=pltpu.SMEM),
                  whole((blk, blk)), whole((blk, blk)), whole((8, blk))],
        out_specs=[pl.BlockSpec((None, blk, blk), lambda h: (h, 0, 0)),
                   pl.BlockSpec((None, blk, blk), lambda h: (h, 0, 0)),
                   pl.BlockSpec((None, 8, blk), lambda h: (h, 0, 0))],
        compiler_params=_cparams(("parallel",)),
        name="rel_bias_tables",
    )(rel_bias, consts["bucket_own"], consts["bucket_prev"], consts["bucket_s"])


def _moba_prompt_body(rb_ref, q_ref, k_ref, v_ref, bown_ref, bprev_ref, *rest, layer):
    o_ref, ko_ref, vo_ref, kbf, vtbf, sem = rest[-6:]
    blk = MOBA_BLOCK
    n_blk = SEQ // blk
    b = pl.program_id(0)
    h = pl.program_id(1)
    scale = C_DIM ** -0.5
    far_bias = rb_ref[REL_BUCKETS - 1, h]

    out_copies = [pltpu.make_async_copy(k_ref, ko_ref.at[layer, b, :, h, :], sem.at[0]),
                  pltpu.make_async_copy(v_ref, vo_ref.at[layer, b, :, h, :], sem.at[1])]
    for prio, c in enumerate(out_copies):
        c.start(priority=prio)

    means = []
    for i in range(n_blk):
        kb = k_ref[i * blk:(i + 1) * blk, :]
        kbf[i * blk:(i + 1) * blk, :] = kb.astype(BF16)
        means.append(jnp.sum(kb, axis=0, keepdims=True) / blk)
        vtbf[:, i * blk:(i + 1) * blk] = v_ref[i * blk:(i + 1) * blk, :].T.astype(BF16)
    means = jnp.concatenate(means + [jnp.zeros((n_blk, C_DIM), F32)], axis=0)
    mh, ml = _split2(means)
    blk_i = lax.broadcasted_iota(jnp.int32, (n_blk, blk), 0)

    for qi in range(n_blk):
        q = q_ref[qi * blk:(qi + 1) * blk, :]
        n_keys = (qi + 1) * blk
        s = lax.dot_general(kbf[0:n_keys, :], q.astype(BF16), _NT, preferred_element_type=F32) * scale
        if qi >= 1:
            qh, ql = _split2(q)
            gate = (lax.dot_general(mh, qh, _NT, preferred_element_type=F32)
                    + lax.dot_general(mh, ql, _NT, preferred_element_type=F32)
                    + lax.dot_general(ml, qh, _NT, preferred_element_type=F32))[0:n_blk]
            rank = jnp.zeros((n_blk, blk), jnp.int32)
            for jp in range(qi):
                gj = gate[jp:jp + 1, :]
                beats = (gj > gate) | ((gj == gate) & (jp < blk_i))
                rank = rank + jnp.where(beats, 1, 0)
            keep = (blk_i < qi) & (rank < MOBA_TOPK)
            far_row = jnp.where(keep, far_bias, NEG_BIG)
            prev_row = jnp.where(keep, 0.0, NEG_BIG)
        pieces = []
        for j in range(qi + 1):
            sj = s[j * blk:(j + 1) * blk]
            if j == qi:
                sj = sj + bown_ref[...]
            elif j == qi - 1:
                sj = sj + bprev_ref[...] + prev_row[j:j + 1]
            else:
                sj = sj + far_row[j:j + 1]
            pieces.append(sj)
        t = jnp.concatenate(pieces, axis=0) if qi else pieces[0]
        m = jnp.max(t, axis=0, keepdims=True)
        p = jnp.exp(t - m)
        l = jnp.sum(p, axis=0, keepdims=True)
        acc = jnp.dot(vtbf[:, 0:n_keys], p.astype(BF16), preferred_element_type=F32)
        o_ref[qi * blk:(qi + 1) * blk, :] = (acc / l).T.astype(o_ref.dtype)

    for c in out_copies:
        c.wait()


def _moba_prompt(proj, kv_out, layer, rel_bias, bown, bprev):
    blk = MOBA_BLOCK
    seq_col = lambda off: pl.BlockSpec((SEQ, C_DIM), lambda b, h: (b, off + h))
    in_specs = [pl.BlockSpec(memory_space=pltpu.SMEM),
                seq_col(OFF_CQ), seq_col(OFF_CK), seq_col(OFF_CV),
                pl.BlockSpec((None, blk, blk), lambda b, h: (h, 0, 0)),
                pl.BlockSpec((None, blk, blk), lambda b, h: (h, 0, 0)),
                pl.BlockSpec(memory_space=pl.ANY), pl.BlockSpec(memory_space=pl.ANY)]
    args = [rel_bias, proj, proj, proj, bown, bprev, *kv_out]
    aliases = {6: 1, 7: 2}
    kv_shape = jax.ShapeDtypeStruct((DEPTH, BATCH, SEQ, C_HEADS, C_DIM), F32)
    yc, k_out, v_out = pl.pallas_call(
        functools.partial(_moba_prompt_body, layer=layer),
        out_shape=[jax.ShapeDtypeStruct((N_TOK_P, C_WIDTH), BF16), kv_shape, kv_shape],
        grid=(BATCH, C_HEADS),
        in_specs=in_specs,
        out_specs=[pl.BlockSpec((SEQ, C_DIM), lambda b, h: (b, h)),
                   pl.BlockSpec(memory_space=pl.ANY), pl.BlockSpec(memory_space=pl.ANY)],
        scratch_shapes=[pltpu.VMEM((SEQ, C_DIM), BF16), pltpu.VMEM((C_DIM, SEQ), BF16),
                        pltpu.SemaphoreType.DMA((2,))],
        input_output_aliases=aliases,
        compiler_params=_cparams(("parallel", "parallel")),
        name="moba_prompt",
    )(*args)
    return yc, (k_out, v_out)


MEANS_BLKS = 4


def _cache_means_body(pt_ref, *refs):
    page_refs, o_ref = refs[:-1], refs[-1]
    for n in range(MEANS_BLKS):
        pages = page_refs[n * PAGES_PER_BLK:(n + 1) * PAGES_PER_BLK]
        tot = functools.reduce(lambda x, y: x + y, [jnp.sum(p[...], axis=0) for p in pages])
        o_ref[n] = tot / MOBA_BLOCK


def _cache_block_means(cache_k, pt_flat):
    def page(i):
        return pl.BlockSpec((None, None, PAGE_SIZE, C_HEADS, C_DIM),
                            lambda l, b, n, pt: (l, pt[b * N_PAGES + n * MEANS_BLKS * PAGES_PER_BLK + i], 0, 0, 0))
    n_pg = MEANS_BLKS * PAGES_PER_BLK
    return pl.pallas_call(
        _cache_means_body,
        out_shape=jax.ShapeDtypeStruct((DEPTH, DEC_BATCH, N_PAST_BLK, C_HEADS, C_DIM), F32),
        grid_spec=pltpu.PrefetchScalarGridSpec(
            num_scalar_prefetch=1,
            grid=(DEPTH, DEC_BATCH, N_PAST_BLK // MEANS_BLKS),
            in_specs=[page(i) for i in range(n_pg)],
            out_specs=pl.BlockSpec((None, None, MEANS_BLKS, C_HEADS, C_DIM),
                                   lambda l, b, n, pt: (l, b, n, 0, 0))),
        compiler_params=_cparams(("parallel", "parallel", "parallel")),
        name="cache_block_means",
    )(pt_flat, *([cache_k] * n_pg))


def _moba_gate_body(q_ref, means_ref, idx_ref):
    n_cand = DEC_BATCH * N_PAST_BLK
    q = q_ref[...]
    means = means_ref[...].reshape(n_cand, C_DIM)
    qh, ql = _split2(q)
    mh, ml = _split2(means)
    g = (lax.dot_general(qh, mh, _NT, preferred_element_type=F32)
         + lax.dot_general(qh, ml, _NT, preferred_element_type=F32)
         + lax.dot_general(ql, mh, _NT, preferred_element_type=F32))
    col = lax.broadcasted_iota(jnp.int32, (N_TOK_S, n_cand), 1)
    row = lax.broadcasted_iota(jnp.int32, (N_TOK_S, n_cand), 0)
    own_seq = (col >> int(math.log2(N_PAST_BLK))) == (row >> int(math.log2(DEC_SEQ)))
    g = jnp.where(own_seq, g, -jnp.inf)
    colf = col.astype(F32)
    lane = lax.broadcasted_iota(jnp.int32, (N_TOK_S, 128), 1)
    out = jnp.zeros((N_TOK_S, 128), jnp.int32)
    for r in range(MOBA_TOPK):
        mx = jnp.max(g, axis=1, keepdims=True)
        am = jnp.min(jnp.where(g == mx, colf, float(n_cand)), axis=1, keepdims=True)
        out = jnp.where(lane == r, am.astype(jnp.int32) & (N_PAST_BLK - 1), out)
        g = jnp.where(colf == am, -jnp.inf, g)
    idx_ref[...] = out


def _moba_gate(proj_s, means, layer):
    return pl.pallas_call(
        _moba_gate_body,
        out_shape=jax.ShapeDtypeStruct((C_HEADS, N_TOK_S, 128), jnp.int32),
        grid=(C_HEADS,),
        in_specs=[pl.BlockSpec((N_TOK_S, C_DIM), lambda h: (0, OFF_CQ + h)),
                  pl.BlockSpec((None, DEC_BATCH, N_PAST_BLK, C_DIM), lambda h: (layer, 0, 0, h))],
        out_specs=pl.BlockSpec((None, N_TOK_S, 128), lambda h: (h, 0, 0)),
        compiler_params=_cparams(("parallel",)),
        name="moba_gate",
    )(proj_s, means.reshape(DEPTH, DEC_BATCH, N_PAST_BLK, C_WIDTH))


def _moba_sample_body(idx_ref, pt_ref, rb_ref, q_ref, kn_ref, vn_ref, bs_ref, ck_ref, cv_ref, o_ref,
                      kbuf, vbuf, sem, *, layer):
    h = pl.program_id(0)
    b = pl.program_id(1)
    step = h * DEC_BATCH + b
    n_steps = C_HEADS * DEC_BATCH
    scale = C_DIM ** -0.5

    def block_of(hh, bb, s, r):
        return idx_ref[(hh * N_TOK_S + bb * DEC_SEQ + s) * MOBA_TOPK + r]

    def page_copies(st, slot):
        hh = st >> int(math.log2(DEC_BATCH))
        bb = st & (DEC_BATCH - 1)
        copies = []
        for s in range(DEC_SEQ):
            for r in range(MOBA_TOPK):
                n = block_of(hh, bb, s, r)
                for j in range(PAGES_PER_BLK):
                    pid = pt_ref[bb * N_PAGES + n * PAGES_PER_BLK + j]
                    i = (s * MOBA_TOPK + r) * PAGES_PER_BLK + j
                    copies.append(pltpu.make_async_copy(ck_ref.at[layer, pid, :, hh, :], kbuf.at[slot, i],
                                                        sem.at[0, slot]))
                    copies.append(pltpu.make_async_copy(cv_ref.at[layer, pid, :, hh, :], vbuf.at[slot, i],
                                                        sem.at[1, slot]))
        return copies

    @pl.when(step == 0)
    def _():
        for c in page_copies(step, 0):
            c.start()

    @pl.when(step + 1 < n_steps)
    def _():
        for c in page_copies(step + 1, (step + 1) & 1):
            c.start()

    slot = step & 1
    for c in page_copies(step, slot):
        c.wait()

    far_bias = rb_ref[REL_BUCKETS - 1, h]
    kn = _pad_frame(kn_ref[...]).astype(BF16)
    vn = _pad_frame(vn_ref[...]).astype(BF16)
    lane = lax.broadcasted_iota(jnp.int32, (1, FRAME), 1)
    ppq = MOBA_TOPK * PAGES_PER_BLK
    n_sel = ppq * PAGE_SIZE
    for s in range(DEC_SEQ):
        row = b * DEC_SEQ + s
        q8 = jnp.broadcast_to(q_ref[s:s + 1, :], (8, C_DIM)).astype(BF16)
        k_sel = kbuf[slot, s * ppq:(s + 1) * ppq].reshape(n_sel, C_DIM).astype(BF16)
        v_sel = vbuf[slot, s * ppq:(s + 1) * ppq].reshape(n_sel, C_DIM).astype(BF16)
        bias = jnp.concatenate(
            [jnp.where(block_of(h, b, s, r) == N_PAST_BLK - 1, bs_ref[s:s + 1, :], far_bias)
             for r in range(MOBA_TOPK)], axis=1)
        l_sel = lax.dot_general(q8, k_sel, _NT, preferred_element_type=F32)[0:1] * scale + bias
        lg = lax.dot_general(q8, kn, _NT, preferred_element_type=F32)[0:1] * scale
        ob = jnp.zeros((1, FRAME), F32)
        for d in range(DEC_SEQ):
            ob = jnp.where(lane == row - d, rb_ref[d, h], ob)
        valid = (lane >= b * DEC_SEQ) & (lane <= row)
        l_own = jnp.where(valid, lg + ob, NEG_BIG)

        mx = jnp.maximum(jnp.max(l_sel, axis=1, keepdims=True), jnp.max(l_own, axis=1, keepdims=True))
        p_sel = jnp.exp(l_sel - mx)
        p_own = jnp.exp(l_own - mx)
        denom = jnp.sum(p_sel, axis=1, keepdims=True) + jnp.sum(p_own, axis=1, keepdims=True)
        out = (jnp.dot(jnp.broadcast_to(p_sel, (8, n_sel)).astype(BF16), v_sel, preferred_element_type=F32)
               + jnp.dot(jnp.broadcast_to(p_own, (8, FRAME)).astype(BF16), vn, preferred_element_type=F32))
        o_ref[s:s + 1, :] = out[0:1] / denom


def _moba_sample(proj_s, cache_k, cache_v, idx_flat, pt_flat, rel_bias, bias_s, layer):
    n_pg = DEC_SEQ * MOBA_TOPK * PAGES_PER_BLK
    proj3 = proj_s.reshape(DEC_BATCH, DEC_SEQ, N_MAIN)
    out = pl.pallas_call(
        functools.partial(_moba_sample_body, layer=layer),
        out_shape=jax.ShapeDtypeStruct((DEC_BATCH, DEC_SEQ, C_WIDTH), F32),
        grid_spec=pltpu.PrefetchScalarGridSpec(
            num_scalar_prefetch=2,
            grid=(C_HEADS, DEC_BATCH),
            in_specs=[pl.BlockSpec(memory_space=pltpu.SMEM),
                      pl.BlockSpec((None, DEC_SEQ, C_DIM), lambda h, b, idx, pt: (b, 0, OFF_CQ + h)),
                      pl.BlockSpec((N_TOK_S, C_DIM), lambda h, b, idx, pt: (0, OFF_CK + h)),
                      pl.BlockSpec((N_TOK_S, C_DIM), lambda h, b, idx, pt: (0, OFF_CV + h)),
                      pl.BlockSpec((None, 8, MOBA_BLOCK), lambda h, b, idx, pt: (h, 0, 0)),
                      pl.BlockSpec(memory_space=pl.ANY),
                      pl.BlockSpec(memory_space=pl.ANY)],
            out_specs=pl.BlockSpec((None, DEC_SEQ, C_DIM), lambda h, b, idx, pt: (b, 0, h)),
            scratch_shapes=[pltpu.VMEM((2, n_pg, PAGE_SIZE, C_DIM), F32),
                            pltpu.VMEM((2, n_pg, PAGE_SIZE, C_DIM), F32),
                            pltpu.SemaphoreType.DMA((2, 2))]),
        compiler_params=_cparams(("arbitrary", "arbitrary")),
        name="moba_sample",
    )(idx_flat, pt_flat, rel_bias, proj3, proj_s, proj_s, bias_s, cache_k, cache_v)
    return out.reshape(N_TOK_S, C_WIDTH)


def _merge_body(ya_ref, yb_ref, yc_ref, wa_ref, wb_ref, wc_ref, g0_ref, g1_ref, g2_ref, o_ref):
    acc = g0_ref[...] * jnp.dot(ya_ref[...], wa_ref[...], preferred_element_type=F32)
    acc = acc + g1_ref[...] * jnp.dot(yb_ref[...], wb_ref[...], preferred_element_type=F32)
    acc = acc + g2_ref[...] * jnp.dot(yc_ref[...], wc_ref[...], preferred_element_type=F32)
    o_ref[...] = acc.astype(o_ref.dtype)


def _merge(ya, yb, yc, gates, w_a, w_b, w_c, layer, *, tm, tn):
    m = ya.shape[0]
    nb = D_MODEL // tn
    act = lambda width: pl.BlockSpec((tm, width), lambda j, i: (i, 0))
    wgt = lambda width: pl.BlockSpec((None, width, tn), lambda j, i: (layer, 0, j))
    gate = lambda br: pl.BlockSpec((tm, tn), lambda j, i: (i, br * nb + j))
    return pl.pallas_call(
        _merge_body,
        out_shape=jax.ShapeDtypeStruct((m, D_MODEL), BF16),
        grid=(nb, m // tm),
        in_specs=[act(A_WIDTH), act(B_WIDTH), act(C_WIDTH), wgt(A_WIDTH), wgt(B_WIDTH), wgt(C_WIDTH),
                  gate(0), gate(1), gate(2)],
        out_specs=pl.BlockSpec((tm, tn), lambda j, i: (i, j)),
        compiler_params=_cparams(("parallel", "parallel")),
        name="branch_merge",
    )(ya, yb, yc, w_a, w_b, w_c, gates, gates, gates)


def kernel(x_prompt, x_sample, cache_k, cache_v, state_hgrn, state_conv, page_table, w_in, b_gate,
           conv_w, hgrn_lb_logits, hgrn_norm_g, w_a_up, w_b_up, w_c_up, w_o, rel_bias, g_pre_mix,
           g_post_mix, g_pre_mlp, g_post_mlp, w_mlp_up, w_mlp_down):
    consts = {k: jnp.asarray(v) for k, v in _constants().items()}
    for name in ("m3_p", "emask_p", "m3_s", "emask_s"):
        consts[name] = consts[name].astype(BF16)

    w_in_b, w_a_b, w_b_b, w_c_b = (w.astype(BF16) for w in (w_in, w_a_up, w_b_up, w_c_up))
    w_o_b, w_up_b, w_dn_b = (w.astype(BF16) for w in (w_o, w_mlp_up, w_mlp_down))
    vec = lambda g: g.reshape(DEPTH, 1, g.shape[-1])
    b_gate3, ng3 = vec(b_gate), vec(hgrn_norm_g)
    gpm, gqm, gpl, gql = vec(g_pre_mix), vec(g_post_mix), vec(g_pre_mlp), vec(g_post_mlp)
    pt_flat = page_table.reshape(-1)

    lbp = _lower_bound_params(hgrn_lb_logits)
    bown, bprev, bias_s = _bias_tables(rel_bias, consts)
    means = _cache_block_means(cache_k, pt_flat)

    mm = dict(tm=1024, tn=1024, tk=D_MODEL)
    mm_down = dict(tm=2048, tn=1024, tk=2048)
    tm_p, tm_s = 256, N_TOK_S
    xp = x_prompt.reshape(N_TOK_P, D_MODEL)
    xs = x_sample.reshape(N_TOK_S, D_MODEL)
    hp = _prenorm(xp, gpm, 0, tm=tm_p)
    hs = _prenorm(xs, gpm, 0, tm=tm_s)
    outs = {k: [] for k in ("ks", "vs", "sp", "ss", "cp", "cs")}
    kv_shape_p = (DEPTH, BATCH, SEQ, C_HEADS, C_DIM)
    kv_p = (jnp.zeros(kv_shape_p, F32), jnp.zeros(kv_shape_p, F32))
    kv_shape_s = (DEC_BATCH, DEC_SEQ, C_HEADS, C_DIM)

    for l in range(DEPTH):
        proj_p, proj_s = _matmul(hp, hs, w_in_b, l, col0=0, n_cols=N_MAIN, name="in_proj", **mm)
        gates_p, gates_s = _matmul(hp, hs, w_in_b, l, col0=N_MAIN, n_cols=N_GATE, epilogue="sigmoid_bias",
                                   bias=b_gate3, name="in_proj_gates", **mm)
        ya_p, st_p = _hgrn_prompt(proj_p, lbp, ng3, l, consts)
        yb_p, cn_p = _shortconv(proj_p.reshape(BATCH, SEQ, N_MAIN), conv_w, l, None, tc=256, out_dtype=BF16)
        yc_p, kv_p = _moba_prompt(proj_p, kv_p, l, rel_bias, bown, bprev)
        ya_s, st_s = _hgrn_sample(proj_s, state_hgrn, lbp, ng3, l, consts)
        yb_s, cn_s = _shortconv(proj_s.reshape(DEC_BATCH, DEC_SEQ, N_MAIN), conv_w, l, state_conv,
                                tc=B_WIDTH, out_dtype=F32)
        idx = _moba_gate(proj_s, means, l)
        idx_flat = idx[:, :, :MOBA_TOPK].reshape(-1)
        yc_s = _moba_sample(proj_s, cache_k, cache_v, idx_flat, pt_flat, rel_bias, bias_s, l)
        outs["ks"].append(proj_s[:, OFF_CK * 128:OFF_CK * 128 + C_WIDTH].reshape(kv_shape_s))
        outs["vs"].append(proj_s[:, OFF_CV * 128:OFF_CV * 128 + C_WIDTH].reshape(kv_shape_s))
        outs["sp"].append(st_p)
        outs["ss"].append(st_s)
        outs["cp"].append(cn_p)
        outs["cs"].append(cn_s)

        merged_p = _merge(ya_p, yb_p.reshape(N_TOK_P, B_WIDTH), yc_p, gates_p, w_a_b, w_b_b, w_c_b, l,
                          tm=512, tn=1024)
        merged_s = _merge(ya_s, yb_s.reshape(N_TOK_S, B_WIDTH).astype(BF16), yc_s.astype(BF16), gates_s,
                          w_a_b, w_b_b, w_c_b, l, tm=N_TOK_S, tn=1024)
        mix_p, mix_s = _matmul(merged_p, merged_s, w_o_b, l, name="out_proj", **mm)
        xp, hm_p = _post(xp, mix_p, gqm, l, gpl, l, tm=tm_p)
        xs, hm_s = _post(xs, mix_s, gqm, l, gpl, l, tm=tm_s)
        up_p, up_s = _matmul(hm_p, hm_s, w_up_b, l, epilogue="relu2", out_dtype=BF16, name="mlp_up", **mm)
        dn_p, dn_s = _matmul(up_p, up_s, w_dn_b, l, name="mlp_down", **mm_down)
        nxt = (gpm, l + 1) if l + 1 < DEPTH else ()
        xp, hp = _post(xp, dn_p, gql, l, *nxt, tm=tm_p)
        xs, hs = _post(xs, dn_s, gql, l, *nxt, tm=tm_s)

    return (xp.reshape(BATCH, SEQ, D_MODEL), xs.reshape(DEC_BATCH, DEC_SEQ, D_MODEL),
            kv_p[0], kv_p[1], jnp.stack(outs["ks"]), jnp.stack(outs["vs"]),
            jnp.stack(outs["sp"]), jnp.stack(outs["ss"]), jnp.stack(outs["cp"]), jnp.stack(outs["cs"]))
```
